```python
import jax, jax.numpy as jnp
from jax import lax
import numpy as np

D_MODEL = 4096
BATCH = 4
SEQ = 2048
DEPTH = 4

N_MIXERS = 2
MLA_HEADS = 32
Q_LORA = 1024
KV_LORA = 512
NOPE_DIM = 128
ROPE_DIM = 64
V_DIM = 128
QK_DIM = NOPE_DIM + ROPE_DIM
MLA_WIDTH = MLA_HEADS * V_DIM
MLA_IN = Q_LORA + KV_LORA + ROPE_DIM + MLA_WIDTH
ROPE_THETA = 10000.0
SB_HEADS = 32
SB_DIM = 128
SB_WIDTH = SB_HEADS * SB_DIM
SB_IN = 4 * SB_WIDTH
Q_BLOCK = 128
EPS = 1e-6
N_MLA = (DEPTH + 1) // 2
N_SB = DEPTH // 2

kernel_name = "hybrid_mla_stickbreaking_gated"


def rms_norm(x, g):
    xf = x.astype(jnp.float32)
    y = xf * lax.rsqrt(jnp.mean(xf * xf, axis=-1, keepdims=True) + EPS)
    return (y * g.astype(jnp.float32)).astype(x.dtype)


def rope_tables(seq):
    inv = 1.0 / (ROPE_THETA ** (jnp.arange(0, ROPE_DIM, 2, dtype=jnp.float32) / ROPE_DIM))
    ang = jnp.arange(seq, dtype=jnp.float32)[:, None] * inv[None, :]
    return jnp.cos(ang), jnp.sin(ang)


def apply_rope(x, cos, sin):
    x1, x2 = jnp.split(x, 2, axis=-1)
    c = cos.astype(x.dtype)
    s = sin.astype(x.dtype)
    return jnp.concatenate([x1 * c - x2 * s, x1 * s + x2 * c], axis=-1)


def to_blocks(t):
    b, s, h, d = t.shape
    return jnp.moveaxis(t.reshape(b, s // Q_BLOCK, Q_BLOCK, h, d), 1, 0)


def from_blocks(t):
    nb, b, qb, h, d = t.shape
    return jnp.moveaxis(t, 0, 1).reshape(b, nb * qb, h * d)


def softmax_attend(q, k, v):
    seq = q.shape[1]
    nb = seq // Q_BLOCK
    scale = QK_DIM ** -0.5
    kpos = jnp.arange(seq)

    def block(args):
        qblk, i = args
        qpos = i * Q_BLOCK + jnp.arange(Q_BLOCK)
        s = jnp.einsum('bqhd,bkhd->bhqk', qblk, k).astype(jnp.float32) * scale
        s = jnp.where(kpos[None, :] <= qpos[:, None], s, -jnp.inf)
        p = jax.nn.softmax(s, axis=-1).astype(v.dtype)
        return jnp.einsum('bhqk,bkhd->bqhd', p, v)

    out = lax.map(block, (to_blocks(q), jnp.arange(nb)))
    return from_blocks(out)


def stick_breaking_attend(q, k, v):
    seq = q.shape[1]
    nb = seq // Q_BLOCK
    scale = SB_DIM ** -0.5
    kpos = jnp.arange(seq)

    def block(args):
        qblk, i = args
        qpos = i * Q_BLOCK + jnp.arange(Q_BLOCK)
        z = jnp.einsum('bqhd,bkhd->bhqk', qblk, k).astype(jnp.float32) * scale
        mask = kpos[None, :] < qpos[:, None]
        log_beta = jax.nn.log_sigmoid(z)
        log_1m = jnp.where(mask, jax.nn.log_sigmoid(-z), 0.0)
        suffix = lax.cumsum(log_1m, axis=3, reverse=True) - log_1m
        w = jnp.where(mask, jnp.exp(log_beta + suffix), 0.0).astype(v.dtype)
        return jnp.einsum('bhqk,bkhd->bqhd', w, v)

    out = lax.map(block, (to_blocks(q), jnp.arange(nb)))
    return from_blocks(out)


def mla_layer(x, norm_g, w_in, q_a_g, w_qb, kv_a_g, w_kvb, q_norm_g, k_norm_g, w_out, cos, sin):
    b, s, _ = x.shape
    h = rms_norm(x, norm_g)
    proj = h @ w_in
    q_lat, kv_lat, k_pe, gate = jnp.split(
        proj, [Q_LORA, Q_LORA + KV_LORA, Q_LORA + KV_LORA + ROPE_DIM], axis=-1)
    q = (rms_norm(q_lat, q_a_g) @ w_qb).reshape(b, s, MLA_HEADS, QK_DIM)
    kv = (rms_norm(kv_lat, kv_a_g) @ w_kvb).reshape(b, s, MLA_HEADS, NOPE_DIM + V_DIM)
    k_nope, v = jnp.split(kv, [NOPE_DIM], axis=-1)
    k_pe_h = jnp.broadcast_to(k_pe[:, :, None, :], (b, s, MLA_HEADS, ROPE_DIM))
    k = jnp.concatenate([k_nope, k_pe_h], axis=-1)
    q = rms_norm(q, q_norm_g)
    k = rms_norm(k, k_norm_g)
    c, sn = cos[:, None, :], sin[:, None, :]
    q = jnp.concatenate([q[..., :NOPE_DIM], apply_rope(q[..., NOPE_DIM:], c, sn)], axis=-1)
    k = jnp.concatenate([k[..., :NOPE_DIM], apply_rope(k[..., NOPE_DIM:], c, sn)], axis=-1)
    o = softmax_attend(q, k, v)
    return x + (o * jax.nn.silu(gate)) @ w_out


def sb_layer(x, norm_g, w_in, w_out):
    b, s, _ = x.shape
    h = rms_norm(x, norm_g)
    q, k, v, gate = jnp.split(h @ w_in, 4, axis=-1)
    q = q.reshape(b, s, SB_HEADS, SB_DIM)
    k = k.reshape(b, s, SB_HEADS, SB_DIM)
    v = v.reshape(b, s, SB_HEADS, SB_DIM)
    o = stick_breaking_attend(q, k, v)
    return x + (o * jax.nn.silu(gate)) @ w_out


def setup_inputs(seed: int = 0) -> dict:
    key = jax.random.key(seed)
    ks = jax.random.split(key, 16)

    def w(k, shape, fan_in):
        return jax.random.normal(k, shape, jnp.float32) * (fan_in ** -0.5)

    def gain(k, shape):
        return 1.0 + 0.02 * jax.random.normal(k, shape, jnp.float32)

    return {
        "x": jax.random.normal(ks[0], (BATCH, SEQ, D_MODEL), jnp.float32),
        "mla_norm_g": gain(ks[1], (N_MLA, D_MODEL)),
        "mla_w_in": w(ks[2], (N_MLA, D_MODEL, MLA_IN), D_MODEL),
        "mla_q_a_g": gain(ks[3], (N_MLA, Q_LORA)),
        "mla_w_qb": w(ks[4], (N_MLA, Q_LORA, MLA_HEADS * QK_DIM), Q_LORA),
        "mla_kv_a_g": gain(ks[5], (N_MLA, KV_LORA)),
        "mla_w_kvb": w(ks[6], (N_MLA, KV_LORA, MLA_HEADS * (NOPE_DIM + V_DIM)), KV_LORA),
        "mla_q_norm_g": gain(ks[7], (N_MLA, QK_DIM)),
        "mla_k_norm_g": gain(ks[8], (N_MLA, QK_DIM)),
        "mla_w_out": w(ks[9], (N_MLA, MLA_WIDTH, D_MODEL), MLA_WIDTH),
        "sb_norm_g": gain(ks[10], (N_SB, D_MODEL)),
        "sb_w_in": w(ks[11], (N_SB, D_MODEL, SB_IN), D_MODEL),
        "sb_w_out": w(ks[12], (N_SB, SB_WIDTH, D_MODEL), SB_WIDTH),
    }


def reference(x, mla_norm_g, mla_w_in, mla_q_a_g, mla_w_qb, mla_kv_a_g, mla_w_kvb,
              mla_q_norm_g, mla_k_norm_g, mla_w_out, sb_norm_g, sb_w_in, sb_w_out):
    cos, sin = rope_tables(x.shape[1])
    h = x
    for i in range(DEPTH):
        j = i // N_MIXERS
        if i % N_MIXERS == 0:
            h = mla_layer(h, mla_norm_g[j], mla_w_in[j], mla_q_a_g[j], mla_w_qb[j],
                          mla_kv_a_g[j], mla_w_kvb[j], mla_q_norm_g[j], mla_k_norm_g[j],
                          mla_w_out[j], cos, sin)
        else:
            h = sb_layer(h, sb_norm_g[j], sb_w_in[j], sb_w_out[j])
    return h
```

```python
import functools

import numpy as np
import jax
import jax.numpy as jnp
from jax import lax
from jax.experimental import pallas as pl
from jax.experimental.pallas import tpu as pltpu

D_MODEL = 4096
BATCH = 4
SEQ = 2048
DEPTH = 4
MLA_HEADS = 32
Q_LORA = 1024
KV_LORA = 512
NOPE_DIM = 128
ROPE_DIM = 64
V_DIM = 128
QK_DIM = NOPE_DIM + ROPE_DIM
MLA_WIDTH = MLA_HEADS * V_DIM
ROPE_THETA = 10000.0
SB_HEADS = 32
SB_DIM = 128
SB_WIDTH = SB_HEADS * SB_DIM
EPS = 1e-6
TOKENS = BATCH * SEQ

V7X_SUBLANES = 8
V7X_LANES = 128
V7X_MXU_DIM = 256
V7X_VMEM_BYTES = 64 * 1024 * 1024
VMEM_LIMIT_CAP = V7X_VMEM_BYTES - 8 * 1024 * 1024

QK_PAD = V7X_MXU_DIM
ATT_BLOCK = 512
N_ATT_BLOCKS = SEQ // ATT_BLOCK

BF16 = jnp.bfloat16
F32 = jnp.float32
NT_DIMS = (((1,), (1,)), ((), ()))


def _vmem_limit(*block_bytes, scratch=0):
    need = 2 * sum(block_bytes) + scratch
    return int(min(VMEM_LIMIT_CAP, need + 8 * 1024 * 1024))


def _nbytes(shape, dtype):
    return int(np.prod(shape)) * jnp.dtype(dtype).itemsize


def _norm_kernel(x_ref, g_ref, o_ref):
    x = x_ref[...]
    ms = jnp.mean(x * x, axis=-1, keepdims=True)
    o_ref[...] = (x * lax.rsqrt(ms + EPS) * g_ref[...]).astype(BF16)


def _norm_perm_kernel(x_ref, g_ref, perm_ref, o_ref, operm_ref):
    x = x_ref[...]
    ms = jnp.mean(x * x, axis=-1, keepdims=True)
    y = (x * lax.rsqrt(ms + EPS) * g_ref[...]).astype(BF16)
    o_ref[...] = y
    operm_ref[...] = jnp.dot(perm_ref[...], y, preferred_element_type=F32).astype(BF16)


def _scan_perm_matrix():
    r = np.arange(ATT_BLOCK)
    src = (r % V7X_SUBLANES) * (ATT_BLOCK // V7X_SUBLANES) + r // V7X_SUBLANES
    p = np.zeros((ATT_BLOCK, ATT_BLOCK), np.float32)
    p[r, src] = 1.0
    return jnp.asarray(p, BF16)


def rms_norm_tm(x, g, *, with_perm=False):
    bt = ATT_BLOCK if with_perm else 256
    grid = (TOKENS // bt,)
    x_spec = pl.BlockSpec((bt, D_MODEL), lambda i: (i, 0))
    g_spec = pl.BlockSpec((1, D_MODEL), lambda i: (0, 0))
    o_spec = pl.BlockSpec((bt, D_MODEL), lambda i: (i, 0))
    o_shape = jax.ShapeDtypeStruct((TOKENS, D_MODEL), BF16)
    xb = _nbytes((bt, D_MODEL), F32)
    ob = _nbytes((bt, D_MODEL), BF16)
    g2 = g.reshape(1, D_MODEL)
    if not with_perm:
        return pl.pallas_call(
            _norm_kernel, grid=grid, in_specs=[x_spec, g_spec], out_specs=o_spec,
            out_shape=o_shape, name="rms_norm",
            compiler_params=pltpu.CompilerParams(vmem_limit_bytes=_vmem_limit(xb, ob, scratch=2 * xb)),
        )(x, g2)
    p_spec = pl.BlockSpec((bt, bt), lambda i: (0, 0))
    return pl.pallas_call(
        _norm_perm_kernel, grid=grid, in_specs=[x_spec, g_spec, p_spec],
        out_specs=[o_spec, o_spec], out_shape=[o_shape, o_shape], name="rms_norm_perm",
        compiler_params=pltpu.CompilerParams(vmem_limit_bytes=_vmem_limit(xb, ob, ob, scratch=3 * xb)),
    )(x, g2, _scan_perm_matrix())


def _epilogue(acc, kind, scale):
    if kind == "silu":
        return (acc * jax.nn.sigmoid(acc)).astype(BF16)
    if kind == "scale":
        return (acc * scale).astype(BF16)
    assert kind == "plain"
    return acc.astype(BF16)


def _mm_nn_kernel(a_ref, b_ref, o_ref, *, kind, scale):
    acc = jnp.dot(a_ref[...], b_ref[...], preferred_element_type=F32)
    o_ref[...] = _epilogue(acc, kind, scale)


def _mm_nn_resid_kernel(a_ref, b_ref, r_ref, o_ref):
    acc = jnp.dot(a_ref[...], b_ref[...], preferred_element_type=F32)
    o_ref[...] = r_ref[...] + acc


def matmul_tm(a, b, *, kind="plain", scale=1.0, bm=1024, bn=1024):
    m, k = a.shape
    _, n = b.shape
    return pl.pallas_call(
        functools.partial(_mm_nn_kernel, kind=kind, scale=scale),
        grid=(m // bm, n // bn),
        in_specs=[pl.BlockSpec((bm, k), lambda i, j: (i, 0)),
                  pl.BlockSpec((k, bn), lambda i, j: (0, j))],
        out_specs=pl.BlockSpec((bm, bn), lambda i, j: (i, j)),
        out_shape=jax.ShapeDtypeStruct((m, n), BF16), name="proj_tm_" + kind,
        compiler_params=pltpu.CompilerParams(vmem_limit_bytes=_vmem_limit(
            _nbytes((bm, k), BF16), _nbytes((k, bn), BF16), _nbytes((bm, bn), BF16),
            scratch=_nbytes((bm, bn), F32))),
    )(a, b)


def matmul_tm_resid(a, b, resid, *, bm=1024, bn=512):
    m, k = a.shape
    _, n = b.shape
    return pl.pallas_call(
        _mm_nn_resid_kernel,
        grid=(m // bm, n // bn),
        in_specs=[pl.BlockSpec((bm, k), lambda i, j: (i, 0)),
                  pl.BlockSpec((k, bn), lambda i, j: (0, j)),
                  pl.BlockSpec((bm, bn), lambda i, j: (i, j))],
        out_specs=pl.BlockSpec((bm, bn), lambda i, j: (i, j)),
        out_shape=jax.ShapeDtypeStruct((m, n), F32), name="proj_out_resid",
        input_output_aliases={2: 0},
        compiler_params=pltpu.CompilerParams(vmem_limit_bytes=_vmem_limit(
            _nbytes((bm, k), BF16), _nbytes((k, bn), BF16), 2 * _nbytes((bm, bn), F32),
            scratch=_nbytes((bm, bn), F32))),
    )(a, b, resid)


def _mm_nt_kernel(w_ref, a_ref, o_ref, *, kind, scale, tblock):
    acc = lax.dot_general(w_ref[...], a_ref[...], NT_DIMS, preferred_element_type=F32)
    out = _epilogue(acc, kind, scale)
    if tblock is None:
        o_ref[...] = out
    else:
        for c in range(out.shape[1] // tblock):
            o_ref[c] = out[:, c * tblock:(c + 1) * tblock]


def matmul_fm(w_t, a, *, kind="plain", scale=1.0, bm=1024, bt=1024, tblock=None):
    n, k = w_t.shape
    t, _ = a.shape
    if tblock is None:
        out_spec = pl.BlockSpec((bm, bt), lambda i, j: (j, i))
        out_shape = jax.ShapeDtypeStruct((n, t), BF16)
    else:
        out_spec = pl.BlockSpec((bt // tblock, bm, tblock), lambda i, j: (i, j, 0))
        out_shape = jax.ShapeDtypeStruct((t // tblock, n, tblock), BF16)
    return pl.pallas_call(
        functools.partial(_mm_nt_kernel, kind=kind, scale=scale, tblock=tblock),
        grid=(t // bt, n // bm),
        in_specs=[pl.BlockSpec((bm, k), lambda i, j: (j, 0)),
                  pl.BlockSpec((bt, k), lambda i, j: (i, 0))],
        out_specs=out_spec, out_shape=out_shape, name="proj_fm_" + kind,
        compiler_params=pltpu.CompilerParams(vmem_limit_bytes=_vmem_limit(
            _nbytes((bm, k), BF16), _nbytes((bt, k), BF16), _nbytes((bm, bt), BF16),
            scratch=_nbytes((bm, bt), F32))),
    )(w_t, a)


def _rope_kernel(inv_ref, cos_ref, sin_ref):
    pos = lax.broadcasted_iota(jnp.int32, cos_ref.shape, 1).astype(F32)
    ang = pos * inv_ref[...]
    cos_ref[...] = jnp.cos(ang)
    sin_ref[...] = jnp.sin(ang)


def rope_tables_fm():
    half = ROPE_DIM // 2
    inv = 1.0 / (np.float32(ROPE_THETA) ** (np.arange(0, ROPE_DIM, 2, dtype=np.float32) / np.float32(ROPE_DIM)))
    inv = jnp.asarray(np.broadcast_to(inv.astype(np.float32)[:, None], (half, SEQ)))
    shape = jax.ShapeDtypeStruct((half, SEQ), F32)
    return pl.pallas_call(_rope_kernel, out_shape=[shape, shape], name="rope_tables")(inv)


def _rope_fm(x, cos, sin):
    half = ROPE_DIM // 2
    x1, x2 = x[:half], x[half:]
    return jnp.concatenate([x1 * cos - x2 * sin, x1 * sin + x2 * cos], axis=0)


def _mla_latent_kernel(hn_ref, wlat_ref, wpe_t_ref, qg_ref, kvg_ref, qn_ref, kvn_ref, kpe_t_ref):
    hn = hn_ref[...]
    acc = jnp.dot(hn, wlat_ref[...], preferred_element_type=F32)
    ql = acc[:, :Q_LORA]
    kvl = acc[:, Q_LORA:]
    qn_ref[...] = (ql * lax.rsqrt(jnp.mean(ql * ql, axis=-1, keepdims=True) + EPS) * qg_ref[...]).astype(BF16)
    kvn_ref[...] = (kvl * lax.rsqrt(jnp.mean(kvl * kvl, axis=-1, keepdims=True) + EPS) * kvg_ref[...]).astype(BF16)
    kpe_t_ref[...] = lax.dot_general(wpe_t_ref[...], hn, NT_DIMS, preferred_element_type=F32)


def mla_latents(hn, w_lat, w_pe_t, q_a_g, kv_a_g, *, bt=512):
    nlat = Q_LORA + KV_LORA
    return pl.pallas_call(
        _mla_latent_kernel, grid=(TOKENS // bt,),
        in_specs=[pl.BlockSpec((bt, D_MODEL), lambda i: (i, 0)),
                  pl.BlockSpec((D_MODEL, nlat), lambda i: (0, 0)),
                  pl.BlockSpec((ROPE_DIM, D_MODEL), lambda i: (0, 0)),
                  pl.BlockSpec((1, Q_LORA), lambda i: (0, 0)),
                  pl.BlockSpec((1, KV_LORA), lambda i: (0, 0))],
        out_specs=[pl.BlockSpec((bt, Q_LORA), lambda i: (i, 0)),
                   pl.BlockSpec((bt, KV_LORA), lambda i: (i, 0)),
                   pl.BlockSpec((ROPE_DIM, bt), lambda i: (0, i))],
        out_shape=[jax.ShapeDtypeStruct((TOKENS, Q_LORA), BF16),
                   jax.ShapeDtypeStruct((TOKENS, KV_LORA), BF16),
                   jax.ShapeDtypeStruct((ROPE_DIM, TOKENS), F32)],
        name="mla_latents",
        compiler_params=pltpu.CompilerParams(vmem_limit_bytes=_vmem_limit(
            _nbytes((bt, D_MODEL), BF16), _nbytes((D_MODEL, nlat), BF16), _nbytes((ROPE_DIM, D_MODEL), BF16),
            _nbytes((bt, nlat), BF16), _nbytes((ROPE_DIM, bt), F32), scratch=2 * _nbytes((bt, nlat), F32))),
    )(hn, w_lat, w_pe_t, q_a_g.reshape(1, Q_LORA), kv_a_g.reshape(1, KV_LORA))


def _mla_q_kernel(w_ref, qn_ref, g_ref, cos_ref, sin_ref, q_t_ref, *, heads, scale):
    acc = lax.dot_general(w_ref[...], qn_ref[...], NT_DIMS, preferred_element_type=F32)
    g = g_ref[...]
    cos = cos_ref[...]
    sin = sin_ref[...]
    bt = acc.shape[1]
    pad = jnp.zeros((QK_PAD - QK_DIM, bt), F32)
    for h in range(heads):
        q = acc[h * QK_DIM:(h + 1) * QK_DIM]
        r = lax.rsqrt(jnp.mean(q * q, axis=0, keepdims=True) + EPS)
        qn = q * r * g
        out = jnp.concatenate([qn[:NOPE_DIM], _rope_fm(qn[NOPE_DIM:], cos, sin), pad], axis=0)
        q_t_ref[h] = (out * scale).astype(BF16)


def mla_q_heads(w_qb_t, qn, q_norm_g, cos_t, sin_t, *, heads=8, bt=512):
    rows = heads * QK_DIM
    g = jnp.broadcast_to(q_norm_g[:, None], (QK_DIM, bt))
    nseq = SEQ // bt
    return pl.pallas_call(
        functools.partial(_mla_q_kernel, heads=heads, scale=QK_DIM ** -0.5),
        grid=(TOKENS // bt, MLA_HEADS // heads),
        in_specs=[pl.BlockSpec((rows, Q_LORA), lambda i, j: (j, 0)),
                  pl.BlockSpec((bt, Q_LORA), lambda i, j: (i, 0)),
                  pl.BlockSpec((QK_DIM, bt), lambda i, j: (0, 0)),
                  pl.BlockSpec((ROPE_DIM // 2, bt), lambda i, j: (0, i % nseq)),
                  pl.BlockSpec((ROPE_DIM // 2, bt), lambda i, j: (0, i % nseq))],
        out_specs=pl.BlockSpec((heads, QK_PAD, bt), lambda i, j: (j, 0, i)),
        out_shape=jax.ShapeDtypeStruct((MLA_HEADS, QK_PAD, TOKENS), BF16), name="mla_q_heads",
        compiler_params=pltpu.CompilerParams(vmem_limit_bytes=_vmem_limit(
            _nbytes((rows, Q_LORA), BF16), _nbytes((bt, Q_LORA), BF16), _nbytes((QK_DIM, bt), F32),
            _nbytes((heads, QK_PAD, bt), BF16), scratch=3 * _nbytes((rows, bt), F32))),
    )(w_qb_t, qn, g, cos_t, sin_t)


def _mla_kv_kernel(w_ref, kvn_ref, kpe_t_ref, g_ref, cos_ref, sin_ref, k_ref, v_t_ref, *, heads):
    acc = lax.dot_general(w_ref[...], kvn_ref[...], NT_DIMS, preferred_element_type=F32)
    g = g_ref[...]
    bt = acc.shape[1]
    kpe = kpe_t_ref[...]
    ss_pe = jnp.sum(kpe * kpe, axis=0, keepdims=True)
    kpe_rot = _rope_fm(kpe * g[NOPE_DIM:], cos_ref[...], sin_ref[...])
    pad = jnp.zeros((QK_PAD - QK_DIM, bt), F32)
    per_head = NOPE_DIM + V_DIM
    for h in range(heads):
        kn = acc[h * per_head:h * per_head + NOPE_DIM]
        v = acc[h * per_head + NOPE_DIM:(h + 1) * per_head]
        ss = jnp.sum(kn * kn, axis=0, keepdims=True) + ss_pe
        r = lax.rsqrt(ss * (1.0 / QK_DIM) + EPS)
        k_t = jnp.concatenate([kn * g[:NOPE_DIM] * r, kpe_rot * r, pad], axis=0)
        k_ref[h] = k_t.T.astype(BF16)
        v_t_ref[h, 0] = v.astype(BF16)


def mla_kv_heads(w_kvb_t, kvn, kpe_t, k_norm_g, cos_t, sin_t, *, heads=4):
    bt = ATT_BLOCK
    rows = heads * (NOPE_DIM + V_DIM)
    g = jnp.broadcast_to(k_norm_g[:, None], (QK_DIM, bt))
    nseq = SEQ // bt
    return pl.pallas_call(
        functools.partial(_mla_kv_kernel, heads=heads),
        grid=(TOKENS // bt, MLA_HEADS // heads),
        in_specs=[pl.BlockSpec((rows, KV_LORA), lambda i, j: (j, 0)),
                  pl.BlockSpec((bt, KV_LORA), lambda i, j: (i, 0)),
                  pl.BlockSpec((ROPE_DIM, bt), lambda i, j: (0, i)),
                  pl.BlockSpec((QK_DIM, bt), lambda i, j: (0, 0)),
                  pl.BlockSpec((ROPE_DIM // 2, bt), lambda i, j: (0, i % nseq)),
                  pl.BlockSpec((ROPE_DIM // 2, bt), lambda i, j: (0, i % nseq))],
        out_specs=[pl.BlockSpec((heads, bt, QK_PAD), lambda i, j: (j, i, 0)),
                   pl.BlockSpec((heads, 1, V_DIM, bt), lambda i, j: (j, i, 0, 0))],
        out_shape=[jax.ShapeDtypeStruct((MLA_HEADS, TOKENS, QK_PAD), BF16),
                   jax.ShapeDtypeStruct((MLA_HEADS, TOKENS // bt, V_DIM, bt), BF16)],
        name="mla_kv_heads",
        compiler_params=pltpu.CompilerParams(vmem_limit_bytes=_vmem_limit(
            _nbytes((rows, KV_LORA), BF16), _nbytes((bt, KV_LORA), BF16), _nbytes((ROPE_DIM, bt), F32),
            _nbytes((QK_DIM, bt), F32), _nbytes((heads, bt, QK_PAD), BF16), _nbytes((heads, V_DIM, bt), BF16),
            scratch=3 * _nbytes((rows, bt), F32))),
    )(w_kvb_t, kvn, kpe_t, g, cos_t, sin_t)


def _gated_store(y_ref, o_t, sg_ref):
    y_ref[...] = (o_t.T * sg_ref[...].astype(F32)).astype(BF16)


def _mla_attn_kernel(q_t_ref, k_ref, v_t_ref, sg_ref, y_ref, m_ref, l_ref, acc_ref):
    blk = ATT_BLOCK
    qi = pl.program_id(2)
    q_t = q_t_ref[...]

    def tile(kj, diagonal):
        start = pl.multiple_of(kj * blk, blk)
        s = jnp.dot(k_ref[pl.ds(start, blk), :], q_t, preferred_element_type=F32)
        if diagonal:
            key = lax.broadcasted_iota(jnp.int32, (blk, blk), 0)
            qry = lax.broadcasted_iota(jnp.int32, (blk, blk), 1)
            s = jnp.where(key <= qry, s, -jnp.inf)
        m_prev = m_ref[...]
        m_new = jnp.maximum(m_prev, jnp.max(s, axis=0, keepdims=True))
        alpha = jnp.exp(m_prev - m_new)
        p = jnp.exp(s - m_new)
        l_ref[...] = alpha * l_ref[...] + jnp.sum(p, axis=0, keepdims=True)
        acc_ref[...] = alpha * acc_ref[...] + jnp.dot(
            v_t_ref[kj], p.astype(BF16), preferred_element_type=F32)
        m_ref[...] = m_new

    m_ref[...] = jnp.full(m_ref.shape, -jnp.inf, F32)
    l_ref[...] = jnp.zeros(l_ref.shape, F32)
    acc_ref[...] = jnp.zeros(acc_ref.shape, F32)
    tile(qi, True)

    def body(kj, carry):
        tile(kj, False)
        return carry

    lax.fori_loop(0, qi, body, 0)
    _gated_store(y_ref, acc_ref[...] / l_ref[...], sg_ref)


def mla_attention(q_t, k, v_t, sg):
    blk = ATT_BLOCK
    nq = N_ATT_BLOCKS
    return pl.pallas_call(
        _mla_attn_kernel, grid=(BATCH, MLA_HEADS, nq),
        in_specs=[pl.BlockSpec((None, QK_PAD, blk), lambda b, h, q: (h, 0, b * nq + q)),
                  pl.BlockSpec((None, SEQ, QK_PAD), lambda b, h, q: (h, b, 0)),
                  pl.BlockSpec((None, nq, V_DIM, blk), lambda b, h, q: (h, b, 0, 0)),
                  pl.BlockSpec((blk, V_DIM), lambda b, h, q: (b * nq + q, h))],
        out_specs=pl.BlockSpec((blk, V_DIM), lambda b, h, q: (b * nq + q, h)),
        out_shape=jax.ShapeDtypeStruct((TOKENS, MLA_WIDTH), BF16),
        scratch_shapes=[pltpu.VMEM((1, blk), F32), pltpu.VMEM((1, blk), F32), pltpu.VMEM((V_DIM, blk), F32)],
        name="mla_attention",
        compiler_params=pltpu.CompilerParams(vmem_limit_bytes=_vmem_limit(
            _nbytes((QK_PAD, blk), BF16), _nbytes((SEQ, QK_PAD), BF16), _nbytes((V_DIM, SEQ), BF16),
            2 * _nbytes((blk, V_DIM), BF16), scratch=8 * _nbytes((blk, blk), F32))),
    )(q_t, k, v_t, sg)


def _sublane_suffix_products(tot):
    sub = lax.broadcasted_iota(jnp.int32, tot.shape, 0)
    one = jnp.ones_like(tot)

    def shifted(x, k):
        return jnp.where(sub < V7X_SUBLANES - k, pltpu.roll(x, V7X_SUBLANES - k, axis=0), one)

    e = shifted(tot, 1)
    e = e * shifted(e, 1)
    e = e * shifted(e, 2)
    e = e * shifted(e, 4)
    full = tot * e
    return e, jnp.broadcast_to(full[0:1], tot.shape)


def _sb_attn_kernel(q_t_ref, k_ref, v_t_ref, sg_ref, y_ref, carry_ref, acc_ref):
    blk = ATT_BLOCK
    groups = blk // V7X_SUBLANES
    qi = pl.program_id(2)
    q_t = q_t_ref[...]

    def tile(kj, diagonal):
        start = pl.multiple_of(kj * blk, blk)
        z = jnp.dot(k_ref[pl.ds(start, blk), :], q_t, preferred_element_type=F32)
        e = jnp.exp(-jnp.abs(z))
        r = 1.0 / (1.0 + e)
        er = e * r
        pos = z >= 0.0
        beta = jnp.where(pos, r, er)
        g = jnp.where(pos, er, r)
        if diagonal:
            row = lax.broadcasted_iota(jnp.int32, (blk, blk), 0)
            key = (row % V7X_SUBLANES) * groups + row // V7X_SUBLANES
            qry = lax.broadcasted_iota(jnp.int32, (blk, blk), 1)
            valid = key < qry
            beta = jnp.where(valid, beta, 0.0)
            g = jnp.where(valid, g, 1.0)
        g3 = g.reshape(groups, V7X_SUBLANES, blk)
        b3 = beta.reshape(groups, V7X_SUBLANES, blk)
        level = [g3[i] for i in range(groups)]
        while len(level) > 1:
            level = [level[i] * level[i + 1] for i in range(0, len(level), 2)]
        later_sublanes, block_total = _sublane_suffix_products(level[0])
        carry = carry_ref[...]
        run = carry * later_sublanes
        w = [None] * groups
        for i in reversed(range(groups)):
            w[i] = b3[i] * run
            run = run * g3[i]
        w = jnp.concatenate(w, axis=0).astype(BF16)
        acc_ref[...] += jnp.dot(v_t_ref[kj], w, preferred_element_type=F32)
        carry_ref[...] = carry * block_total

    carry_ref[...] = jnp.ones(carry_ref.shape, F32)
    acc_ref[...] = jnp.zeros(acc_ref.shape, F32)
    tile(qi, True)

    def body(it, c):
        tile(qi - 1 - it, False)
        return c

    lax.fori_loop(0, qi, body, 0)
    _gated_store(y_ref, acc_ref[...], sg_ref)


def sb_attention(q_t, k, v_t, sg):
    blk = ATT_BLOCK
    nq = N_ATT_BLOCKS
    return pl.pallas_call(
        _sb_attn_kernel, grid=(BATCH, SB_HEADS, nq),
        in_specs=[pl.BlockSpec((SB_DIM, blk), lambda b, h, q: (h, b * nq + q)),
                  pl.BlockSpec((SEQ, SB_DIM), lambda b, h, q: (b, h)),
                  pl.BlockSpec((nq, SB_DIM, blk), lambda b, h, q: (b, h, 0)),
                  pl.BlockSpec((blk, SB_DIM), lambda b, h, q: (b * nq + q, h))],
        out_specs=pl.BlockSpec((blk, SB_DIM), lambda b, h, q: (b * nq + q, h)),
        out_shape=jax.ShapeDtypeStruct((TOKENS, SB_WIDTH), BF16),
        scratch_shapes=[pltpu.VMEM((V7X_SUBLANES, blk), F32), pltpu.VMEM((SB_DIM, blk), F32)],
        name="sb_attention",
        compiler_params=pltpu.CompilerParams(vmem_limit_bytes=_vmem_limit(
            _nbytes((SB_DIM, blk), BF16), _nbytes((SEQ, SB_DIM), BF16), _nbytes((SB_DIM, SEQ), BF16),
            2 * _nbytes((blk, SB_DIM), BF16), scratch=10 * _nbytes((blk, blk), F32))),
    )(q_t, k, v_t, sg)


def _mla_layer(h, norm_g, w_in, q_a_g, w_qb, kv_a_g, w_kvb, q_norm_g, k_norm_g, w_out, cos_t, sin_t):
    nlat = Q_LORA + KV_LORA
    w_lat = w_in[:, :nlat].astype(BF16)
    w_pe_t = w_in[:, nlat:nlat + ROPE_DIM].T.astype(BF16)
    w_gate = w_in[:, nlat + ROPE_DIM:].astype(BF16)
    hn = rms_norm_tm(h, norm_g)
    sg = matmul_tm(hn, w_gate, kind="silu")
    qn, kvn, kpe_t = mla_latents(hn, w_lat, w_pe_t, q_a_g, kv_a_g)
    q_t = mla_q_heads(w_qb.T.astype(BF16), qn, q_norm_g, cos_t, sin_t)
    k, v_t = mla_kv_heads(w_kvb.T.astype(BF16), kvn, kpe_t, k_norm_g, cos_t, sin_t)
    y = mla_attention(q_t, k, v_t, sg)
    return matmul_tm_resid(y, w_out.astype(BF16), h)


def _sb_layer(h, norm_g, w_in, w_out):
    w = SB_WIDTH
    hn, hn_perm = rms_norm_tm(h, norm_g, with_perm=True)
    q_t = matmul_fm(w_in[:, :w].T.astype(BF16), hn, kind="scale", scale=SB_DIM ** -0.5)
    k = matmul_tm(hn_perm, w_in[:, w:2 * w].astype(BF16))
    v_t = matmul_fm(w_in[:, 2 * w:3 * w].T.astype(BF16), hn_perm, tblock=ATT_BLOCK)
    sg = matmul_tm(hn, w_in[:, 3 * w:].astype(BF16), kind="silu")
    y = sb_attention(q_t, k, v_t, sg)
    return matmul_tm_resid(y, w_out.astype(BF16), h)


def kernel(x, mla_norm_g, mla_w_in, mla_q_a_g, mla_w_qb, mla_kv_a_g, mla_w_kvb, mla_q_norm_g,
           mla_k_norm_g, mla_w_out, sb_norm_g, sb_w_in, sb_w_out):
    cos_t, sin_t = rope_tables_fm()
    h = x.reshape(TOKENS, D_MODEL)
    for i in range(DEPTH):
        j = i // 2
        if i % 2 == 0:
            h = _mla_layer(h, mla_norm_g[j], mla_w_in[j], mla_q_a_g[j], mla_w_qb[j], mla_kv_a_g[j],
                           mla_w_kvb[j], mla_q_norm_g[j], mla_k_norm_g[j], mla_w_out[j], cos_t, sin_t)
        else:
            h = _sb_layer(h, sb_norm_g[j], sb_w_in[j], sb_w_out[j])
    return h.reshape(BATCH, SEQ, D_MODEL)
```

```python
import functools

import numpy as np
import jax
import jax.numpy as jnp
from jax import lax
from jax.experimental import pallas as pl
from jax.experimental.pallas import tpu as pltpu

D_MODEL = 4096
BATCH = 4
SEQ = 2048
DEPTH = 4
MLA_HEADS = 32
Q_LORA = 1024
KV_LORA = 512
NOPE_DIM = 128
ROPE_DIM = 64
V_DIM = 128
QK_DIM = NOPE_DIM + ROPE_DIM
MLA_WIDTH = MLA_HEADS * V_DIM
ROPE_THETA = 10000.0
SB_HEADS = 32
SB_DIM = 128
SB_WIDTH = SB_HEADS * SB_DIM
EPS = 1e-6
TOKENS = BATCH * SEQ

V7X_SUBLANES = 8
V7X_LANES = 128
V7X_MXU_DIM = 256
V7X_VMEM_BYTES = 64 * 1024 * 1024
VMEM_LIMIT_CAP = V7X_VMEM_BYTES - 8 * 1024 * 1024

QK_PAD = V7X_MXU_DIM
ATT_BLOCK = 512
N_ATT_BLOCKS = SEQ // ATT_BLOCK

BF16 = jnp.bfloat16
F32 = jnp.float32
NT_DIMS = (((1,), (1,)), ((), ()))


def _vmem_limit(*block_bytes, scratch=0):
    need = 2 * sum(block_bytes) + scratch
    return int(min(VMEM_LIMIT_CAP, need + 8 * 1024 * 1024))


def _nbytes(shape, dtype):
    return int(np.prod(shape)) * jnp.dtype(dtype).itemsize


def _norm_kernel(x_ref, g_ref, o_ref):
    x = x_ref[...]
    ms = jnp.mean(x * x, axis=-1, keepdims=True)
    o_ref[...] = (x * lax.rsqrt(ms + EPS) * g_ref[...]).astype(BF16)


def _norm_perm_kernel(x_ref, g_ref, perm_ref, o_ref, operm_ref):
    x = x_ref[...]
    ms = jnp.mean(x * x, axis=-1, keepdims=True)
    y = (x * lax.rsqrt(ms + EPS) * g_ref[...]).astype(BF16)
    o_ref[...] = y
    operm_ref[...] = jnp.dot(perm_ref[...], y, preferred_element_type=F32).astype(BF16)


def _scan_perm_matrix():
    r = np.arange(ATT_BLOCK)
    src = (r % V7X_SUBLANES) * (ATT_BLOCK // V7X_SUBLANES) + r // V7X_SUBLANES
    p = np.zeros((ATT_BLOCK, ATT_BLOCK), np.float32)
    p[r, src] = 1.0
    return jnp.asarray(p, BF16)


def rms_norm_tm(x, g, *, with_perm=False):
    bt = ATT_BLOCK if with_perm else 256
    grid = (TOKENS // bt,)
    x_spec = pl.BlockSpec((bt, D_MODEL), lambda i: (i, 0))
    g_spec = pl.BlockSpec((1, D_MODEL), lambda i: (0, 0))
    o_spec = pl.BlockSpec((bt, D_MODEL), lambda i: (i, 0))
    o_shape = jax.ShapeDtypeStruct((TOKENS, D_MODEL), BF16)
    xb = _nbytes((bt, D_MODEL), F32)
    ob = _nbytes((bt, D_MODEL), BF16)
    g2 = g.reshape(1, D_MODEL)
    if not with_perm:
        return pl.pallas_call(
            _norm_kernel, grid=grid, in_specs=[x_spec, g_spec], out_specs=o_spec,
            out_shape=o_shape, name="rms_norm",
            compiler_params=pltpu.CompilerParams(vmem_limit_bytes=_vmem_limit(xb, ob, scratch=2 * xb)),
        )(x, g2)
    p_spec = pl.BlockSpec((bt, bt), lambda i: (0, 0))
    return pl.pallas_call(
        _norm_perm_kernel, grid=grid, in_specs=[x_spec, g_spec, p_spec],
        out_specs=[o_spec, o_spec], out_shape=[o_shape, o_shape], name="rms_norm_perm",
        compiler_params=pltpu.CompilerParams(vmem_limit_bytes=_vmem_limit(xb, ob, ob, scratch=3 * xb)),
    )(x, g2, _scan_perm_matrix())


def _proj_kernel(*refs, kind, scale, tblock, cast):
    if kind == "resid":
        a_ref, w_ref, r_ref, o_ref = refs[:4]
        scratch = refs[4:]
    else:
        a_ref, w_ref, o_ref = refs[:3]
        scratch = refs[3:]
    if cast:
        (w_bf_ref,) = scratch

        @pl.when(pl.program_id(1) == 0)
        def _cast_weights():
            w_bf_ref[...] = w_ref[...].astype(BF16)

        w = w_bf_ref[...]
    else:
        w = w_ref[...]
    acc = jnp.dot(a_ref[...], w, preferred_element_type=F32)
    if kind == "resid":
        o_ref[...] = r_ref[...] + acc
    elif kind == "silu":
        o_ref[...] = (acc * jax.nn.sigmoid(acc)).astype(BF16)
    elif kind == "plain":
        o_ref[...] = acc.astype(BF16)
    else:
        assert kind == "fm"
        out = (acc * scale).T.astype(BF16) if scale != 1.0 else acc.T.astype(BF16)
        if tblock is None:
            o_ref[...] = out
        else:
            for c in range(out.shape[1] // tblock):
                o_ref[c] = out[:, c * tblock:(c + 1) * tblock]


def project(a, w, *, kind, layer=None, col0=0, n=None, scale=1.0, tblock=None, resid=None, bm=1024, bn=512):
    m, k = a.shape
    cast = w.ndim == 3
    if cast:
        assert col0 % bn == 0
        w_spec = pl.BlockSpec((None, k, bn), lambda j, i: (layer, 0, col0 // bn + j))
    else:
        n = w.shape[1]
        w_spec = pl.BlockSpec((k, bn), lambda j, i: (0, j))
    in_specs = [pl.BlockSpec((bm, k), lambda j, i: (i, 0)), w_spec]
    args = [a, w]
    out_bytes = _nbytes((bm, bn), BF16)
    aliases = {}
    if kind == "resid":
        in_specs.append(pl.BlockSpec((bm, bn), lambda j, i: (i, j)))
        args.append(resid)
        out_spec = pl.BlockSpec((bm, bn), lambda j, i: (i, j))
        out_shape = jax.ShapeDtypeStruct((m, n), F32)
        out_bytes = 2 * _nbytes((bm, bn), F32)
        aliases = {2: 0}
    elif kind == "fm" and tblock is not None:
        out_spec = pl.BlockSpec((bm // tblock, bn, tblock), lambda j, i: (i, j, 0))
        out_shape = jax.ShapeDtypeStruct((m // tblock, n, tblock), BF16)
    elif kind == "fm":
        out_spec = pl.BlockSpec((bn, bm), lambda j, i: (j, i))
        out_shape = jax.ShapeDtypeStruct((n, m), BF16)
    else:
        out_spec = pl.BlockSpec((bm, bn), lambda j, i: (i, j))
        out_shape = jax.ShapeDtypeStruct((m, n), BF16)
    return pl.pallas_call(
        functools.partial(_proj_kernel, kind=kind, scale=scale, tblock=tblock, cast=cast),
        grid=(n // bn, m // bm), in_specs=in_specs, out_specs=out_spec, out_shape=out_shape,
        scratch_shapes=[pltpu.VMEM((k, bn), BF16)] if cast else [],
        input_output_aliases=aliases, name="proj_" + kind,
        compiler_params=pltpu.CompilerParams(vmem_limit_bytes=_vmem_limit(
            _nbytes((bm, k), BF16), _nbytes((k, bn), F32 if cast else BF16), out_bytes,
            scratch=_nbytes((k, bn), BF16) + 3 * _nbytes((bm, bn), F32))),
    )(*args)


def _rope_kernel(inv_ref, cos_ref, sin_ref):
    pos = lax.broadcasted_iota(jnp.int32, cos_ref.shape, 1).astype(F32)
    ang = pos * inv_ref[...]
    cos_ref[...] = jnp.cos(ang)
    sin_ref[...] = jnp.sin(ang)


def rope_tables_fm():
    half = ROPE_DIM // 2
    inv = 1.0 / (np.float32(ROPE_THETA) ** (np.arange(0, ROPE_DIM, 2, dtype=np.float32) / np.float32(ROPE_DIM)))
    inv = jnp.asarray(np.broadcast_to(inv.astype(np.float32)[:, None], (half, SEQ)))
    shape = jax.ShapeDtypeStruct((half, SEQ), F32)
    return pl.pallas_call(_rope_kernel, out_shape=[shape, shape], name="rope_tables")(inv)


def _rope_fm(x, cos, sin):
    half = ROPE_DIM // 2
    x1, x2 = x[:half], x[half:]
    return jnp.concatenate([x1 * cos - x2 * sin, x1 * sin + x2 * cos], axis=0)


def _mla_latent_kernel(hn_ref, wlat_ref, wpe_t_ref, qg_ref, kvg_ref, qn_ref, kvn_ref, kpe_t_ref):
    hn = hn_ref[...]
    acc = jnp.dot(hn, wlat_ref[...], preferred_element_type=F32)
    ql = acc[:, :Q_LORA]
    kvl = acc[:, Q_LORA:]
    qn_ref[...] = (ql * lax.rsqrt(jnp.mean(ql * ql, axis=-1, keepdims=True) + EPS) * qg_ref[...]).astype(BF16)
    kvn_ref[...] = (kvl * lax.rsqrt(jnp.mean(kvl * kvl, axis=-1, keepdims=True) + EPS) * kvg_ref[...]).astype(BF16)
    kpe_t_ref[...] = lax.dot_general(wpe_t_ref[...], hn, NT_DIMS, preferred_element_type=F32)


def mla_latents(hn, w_lat, w_pe_t, q_a_g, kv_a_g, *, bt=512):
    nlat = Q_LORA + KV_LORA
    return pl.pallas_call(
        _mla_latent_kernel, grid=(TOKENS // bt,),
        in_specs=[pl.BlockSpec((bt, D_MODEL), lambda i: (i, 0)),
                  pl.BlockSpec((D_MODEL, nlat), lambda i: (0, 0)),
                  pl.BlockSpec((ROPE_DIM, D_MODEL), lambda i: (0, 0)),
                  pl.BlockSpec((1, Q_LORA), lambda i: (0, 0)),
                  pl.BlockSpec((1, KV_LORA), lambda i: (0, 0))],
        out_specs=[pl.BlockSpec((bt, Q_LORA), lambda i: (i, 0)),
                   pl.BlockSpec((bt, KV_LORA), lambda i: (i, 0)),
                   pl.BlockSpec((ROPE_DIM, bt), lambda i: (0, i))],
        out_shape=[jax.ShapeDtypeStruct((TOKENS, Q_LORA), BF16),
                   jax.ShapeDtypeStruct((TOKENS, KV_LORA), BF16),
                   jax.ShapeDtypeStruct((ROPE_DIM, TOKENS), F32)],
        name="mla_latents",
        compiler_params=pltpu.CompilerParams(vmem_limit_bytes=_vmem_limit(
            _nbytes((bt, D_MODEL), BF16), _nbytes((D_MODEL, nlat), BF16), _nbytes((ROPE_DIM, D_MODEL), BF16),
            _nbytes((bt, nlat), BF16), _nbytes((ROPE_DIM, bt), F32), scratch=2 * _nbytes((bt, nlat), F32))),
    )(hn, w_lat, w_pe_t, q_a_g.reshape(1, Q_LORA), kv_a_g.reshape(1, KV_LORA))


def _mla_q_kernel(w_ref, qn_ref, g_ref, cos_ref, sin_ref, q_t_ref, *, heads, scale):
    acc = lax.dot_general(w_ref[...], qn_ref[...], NT_DIMS, preferred_element_type=F32)
    g = g_ref[...]
    cos = cos_ref[...]
    sin = sin_ref[...]
    bt = acc.shape[1]
    pad = jnp.zeros((QK_PAD - QK_DIM, bt), F32)
    for h in range(heads):
        q = acc[h * QK_DIM:(h + 1) * QK_DIM]
        r = lax.rsqrt(jnp.mean(q * q, axis=0, keepdims=True) + EPS)
        qn = q * r * g
        out = jnp.concatenate([qn[:NOPE_DIM], _rope_fm(qn[NOPE_DIM:], cos, sin), pad], axis=0)
        q_t_ref[h] = (out * scale).astype(BF16)


def mla_q_heads(w_qb_t, qn, q_norm_g, cos_t, sin_t, *, heads=8, bt=512):
    rows = heads * QK_DIM
    g = jnp.broadcast_to(q_norm_g[:, None], (QK_DIM, bt))
    nseq = SEQ // bt
    return pl.pallas_call(
        functools.partial(_mla_q_kernel, heads=heads, scale=QK_DIM ** -0.5 * np.log2(np.e)),
        grid=(TOKENS // bt, MLA_HEADS // heads),
        in_specs=[pl.BlockSpec((rows, Q_LORA), lambda i, j: (j, 0)),
                  pl.BlockSpec((bt, Q_LORA), lambda i, j: (i, 0)),
                  pl.BlockSpec((QK_DIM, bt), lambda i, j: (0, 0)),
                  pl.BlockSpec((ROPE_DIM // 2, bt), lambda i, j: (0, i % nseq)),
                  pl.BlockSpec((ROPE_DIM // 2, bt), lambda i, j: (0, i % nseq))],
        out_specs=pl.BlockSpec((heads, QK_PAD, bt), lambda i, j: (j, 0, i)),
        out_shape=jax.ShapeDtypeStruct((MLA_HEADS, QK_PAD, TOKENS), BF16), name="mla_q_heads",
        compiler_params=pltpu.CompilerParams(vmem_limit_bytes=_vmem_limit(
            _nbytes((rows, Q_LORA), BF16), _nbytes((bt, Q_LORA), BF16), _nbytes((QK_DIM, bt), F32),
            _nbytes((heads, QK_PAD, bt), BF16), scratch=3 * _nbytes((rows, bt), F32))),
    )(w_qb_t, qn, g, cos_t, sin_t)


def _mla_kv_kernel(w_ref, kvn_ref, kpe_t_ref, g_ref, cos_ref, sin_ref, k_ref, v_t_ref, *, heads):
    acc = lax.dot_general(w_ref[...], kvn_ref[...], NT_DIMS, preferred_element_type=F32)
    g = g_ref[...]
    bt = acc.shape[1]
    kpe = kpe_t_ref[...]
    ss_pe = jnp.sum(kpe * kpe, axis=0, keepdims=True)
    kpe_rot = _rope_fm(kpe * g[NOPE_DIM:], cos_ref[...], sin_ref[...])
    pad = jnp.zeros((QK_PAD - QK_DIM, bt), F32)
    per_head = NOPE_DIM + V_DIM
    for h in range(heads):
        kn = acc[h * per_head:h * per_head + NOPE_DIM]
        v = acc[h * per_head + NOPE_DIM:(h + 1) * per_head]
        ss = jnp.sum(kn * kn, axis=0, keepdims=True) + ss_pe
        r = lax.rsqrt(ss * (1.0 / QK_DIM) + EPS)
        k_t = jnp.concatenate([kn * g[:NOPE_DIM] * r, kpe_rot * r, pad], axis=0)
        k_ref[h] = k_t.T.astype(BF16)
        v_t_ref[h, 0] = v.astype(BF16)


def mla_kv_heads(w_kvb_t, kvn, kpe_t, k_norm_g, cos_t, sin_t, *, heads=4):
    bt = ATT_BLOCK
    rows = heads * (NOPE_DIM + V_DIM)
    g = jnp.broadcast_to(k_norm_g[:, None], (QK_DIM, bt))
    nseq = SEQ // bt
    return pl.pallas_call(
        functools.partial(_mla_kv_kernel, heads=heads),
        grid=(TOKENS // bt, MLA_HEADS // heads),
        in_specs=[pl.BlockSpec((rows, KV_LORA), lambda i, j: (j, 0)),
                  pl.BlockSpec((bt, KV_LORA), lambda i, j: (i, 0)),
                  pl.BlockSpec((ROPE_DIM, bt), lambda i, j: (0, i)),
                  pl.BlockSpec((QK_DIM, bt), lambda i, j: (0, 0)),
                  pl.BlockSpec((ROPE_DIM // 2, bt), lambda i, j: (0, i % nseq)),
                  pl.BlockSpec((ROPE_DIM // 2, bt), lambda i, j: (0, i % nseq))],
        out_specs=[pl.BlockSpec((heads, bt, QK_PAD), lambda i, j: (j, i, 0)),
                   pl.BlockSpec((heads, 1, V_DIM, bt), lambda i, j: (j, i, 0, 0))],
        out_shape=[jax.ShapeDtypeStruct((MLA_HEADS, TOKENS, QK_PAD), BF16),
                   jax.ShapeDtypeStruct((MLA_HEADS, TOKENS // bt, V_DIM, bt), BF16)],
        name="mla_kv_heads",
        compiler_params=pltpu.CompilerParams(vmem_limit_bytes=_vmem_limit(
            _nbytes((rows, KV_LORA), BF16), _nbytes((bt, KV_LORA), BF16), _nbytes((ROPE_DIM, bt), F32),
            _nbytes((QK_DIM, bt), F32), _nbytes((heads, bt, QK_PAD), BF16), _nbytes((heads, V_DIM, bt), BF16),
            scratch=3 * _nbytes((rows, bt), F32))),
    )(w_kvb_t, kvn, kpe_t, g, cos_t, sin_t)


def _gated_store(y_ref, qi, o_t, sg_ref):
    rows = slice(qi * ATT_BLOCK, (qi + 1) * ATT_BLOCK)
    y_ref[rows, :] = (o_t.T * sg_ref[rows, :].astype(F32)).astype(BF16)


def _mla_attn_kernel(q_t_ref, k_ref, v_t_ref, sg_ref, y_ref):
    blk = ATT_BLOCK
    key = lax.broadcasted_iota(jnp.int32, (blk, blk), 0)
    qry = lax.broadcasted_iota(jnp.int32, (blk, blk), 1)
    for qi in range(N_ATT_BLOCKS):
        q_t = q_t_ref[:, qi * blk:(qi + 1) * blk]
        m = l = acc = None
        for kj in [qi] + list(range(qi)):
            s = jnp.dot(k_ref[kj * blk:(kj + 1) * blk, :], q_t, preferred_element_type=F32)
            if kj == qi:
                s = jnp.where(key <= qry, s, -jnp.inf)
            m_tile = jnp.max(s, axis=0, keepdims=True)
            m_new = m_tile if m is None else jnp.maximum(m, m_tile)
            p = jnp.exp2(s - m_new)
            l_tile = jnp.sum(p, axis=0, keepdims=True)
            pv = jnp.dot(v_t_ref[kj], p.astype(BF16), preferred_element_type=F32)
            if m is None:
                l, acc = l_tile, pv
            else:
                alpha = jnp.exp2(m - m_new)
                l = alpha * l + l_tile
                acc = alpha * acc + pv
            m = m_new
        _gated_store(y_ref, qi, acc / l, sg_ref)


def mla_attention(q_t, k, v_t, sg):
    blk = ATT_BLOCK
    nq = N_ATT_BLOCKS
    return pl.pallas_call(
        _mla_attn_kernel, grid=(BATCH, MLA_HEADS),
        in_specs=[pl.BlockSpec((None, QK_PAD, SEQ), lambda b, h: (h, 0, b)),
                  pl.BlockSpec((None, SEQ, QK_PAD), lambda b, h: (h, b, 0)),
                  pl.BlockSpec((None, nq, V_DIM, blk), lambda b, h: (h, b, 0, 0)),
                  pl.BlockSpec((SEQ, V_DIM), lambda b, h: (b, h))],
        out_specs=pl.BlockSpec((SEQ, V_DIM), lambda b, h: (b, h)),
        out_shape=jax.ShapeDtypeStruct((TOKENS, MLA_WIDTH), BF16),
        name="mla_attention",
        compiler_params=pltpu.CompilerParams(vmem_limit_bytes=_vmem_limit(
            _nbytes((QK_PAD, SEQ), BF16), _nbytes((SEQ, QK_PAD), BF16), _nbytes((V_DIM, SEQ), BF16),
            2 * _nbytes((SEQ, V_DIM), BF16), scratch=24 * _nbytes((blk, blk), F32))),
    )(q_t, k, v_t, sg)


def _sublane_suffix_products(tot):
    sub = lax.broadcasted_iota(jnp.int32, tot.shape, 0)
    one = jnp.ones_like(tot)

    def shifted(x, k):
        return jnp.where(sub < V7X_SUBLANES - k, pltpu.roll(x, V7X_SUBLANES - k, axis=0), one)

    e = shifted(tot, 1)
    e = e * shifted(e, 1)
    e = e * shifted(e, 2)
    e = e * shifted(e, 4)
    full = tot * e
    return e, jnp.broadcast_to(full[0:1], tot.shape)


def _sb_attn_kernel(q_t_ref, k_ref, v_t_ref, sg_ref, y_ref):
    blk = ATT_BLOCK
    groups = blk // V7X_SUBLANES
    row = lax.broadcasted_iota(jnp.int32, (blk, blk), 0)
    key = (row % V7X_SUBLANES) * groups + row // V7X_SUBLANES
    valid = key < lax.broadcasted_iota(jnp.int32, (blk, blk), 1)
    for qi in range(N_ATT_BLOCKS):
        q_t = q_t_ref[:, qi * blk:(qi + 1) * blk]
        carry = acc = None
        for kj in range(qi, -1, -1):
            z2 = jnp.dot(k_ref[kj * blk:(kj + 1) * blk, :], q_t, preferred_element_type=F32)
            g = 1.0 / (1.0 + jnp.exp2(z2))
            if kj == qi:
                g = jnp.where(valid, g, 1.0)
            beta = 1.0 - g
            g3 = g.reshape(groups, V7X_SUBLANES, blk)
            b3 = beta.reshape(groups, V7X_SUBLANES, blk)
            level = [g3[i] for i in range(groups)]
            while len(level) > 1:
                level = [level[i] * level[i + 1] for i in range(0, len(level), 2)]
            later_sublanes, block_total = _sublane_suffix_products(level[0])
            run = later_sublanes if carry is None else carry * later_sublanes
            w = [None] * groups
            for i in reversed(range(groups)):
                w[i] = b3[i] * run
                run = run * g3[i]
            w = jnp.concatenate(w, axis=0).astype(BF16)
            pv = jnp.dot(v_t_ref[kj], w, preferred_element_type=F32)
            acc = pv if acc is None else acc + pv
            carry = block_total if carry is None else carry * block_total
        _gated_store(y_ref, qi, acc, sg_ref)


def sb_attention(q_t, k, v_t, sg):
    blk = ATT_BLOCK
    nq = N_ATT_BLOCKS
    return pl.pallas_call(
        _sb_attn_kernel, grid=(BATCH, SB_HEADS),
        in_specs=[pl.BlockSpec((SB_DIM, SEQ), lambda b, h: (h, b)),
                  pl.BlockSpec((SEQ, SB_DIM), lambda b, h: (b, h)),
                  pl.BlockSpec((nq, SB_DIM, blk), lambda b, h: (b, h, 0)),
                  pl.BlockSpec((SEQ, SB_DIM), lambda b, h: (b, h))],
        out_specs=pl.BlockSpec((SEQ, SB_DIM), lambda b, h: (b, h)),
        out_shape=jax.ShapeDtypeStruct((TOKENS, SB_WIDTH), BF16),
        name="sb_attention",
        compiler_params=pltpu.CompilerParams(vmem_limit_bytes=_vmem_limit(
            _nbytes((SB_DIM, SEQ), BF16), _nbytes((SEQ, SB_DIM), BF16), _nbytes((SB_DIM, SEQ), BF16),
            2 * _nbytes((SEQ, SB_DIM), BF16), scratch=24 * _nbytes((blk, blk), F32))),
    )(q_t, k, v_t, sg)


def _mla_layer(h, j, norm_g, w_in, q_a_g, w_qb, kv_a_g, w_kvb, q_norm_g, k_norm_g, w_out_all, cos_t, sin_t):
    nlat = Q_LORA + KV_LORA
    w_lat = w_in[:, :nlat].astype(BF16)
    w_pe_t = w_in[:, nlat:nlat + ROPE_DIM].T.astype(BF16)
    w_gate = w_in[:, nlat + ROPE_DIM:].astype(BF16)
    hn = rms_norm_tm(h, norm_g)
    sg = project(hn, w_gate, kind="silu")
    qn, kvn, kpe_t = mla_latents(hn, w_lat, w_pe_t, q_a_g, kv_a_g)
    q_t = mla_q_heads(w_qb.T.astype(BF16), qn, q_norm_g, cos_t, sin_t)
    k, v_t = mla_kv_heads(w_kvb.T.astype(BF16), kvn, kpe_t, k_norm_g, cos_t, sin_t)
    y = mla_attention(q_t, k, v_t, sg)
    return project(y, w_out_all, layer=j, n=D_MODEL, kind="resid", resid=h)


def _sb_layer(h, j, norm_g, w_in_all, w_out_all):
    w = SB_WIDTH
    hn, hn_perm = rms_norm_tm(h, norm_g, with_perm=True)
    q_t = project(hn, w_in_all, layer=j, col0=0, n=w, kind="fm", scale=SB_DIM ** -0.5 * np.log2(np.e))
    k = project(hn_perm, w_in_all, layer=j, col0=w, n=w, kind="plain")
    v_t = project(hn_perm, w_in_all, layer=j, col0=2 * w, n=w, kind="fm", tblock=ATT_BLOCK)
    sg = project(hn, w_in_all, layer=j, col0=3 * w, n=w, kind="silu")
    y = sb_attention(q_t, k, v_t, sg)
    return project(y, w_out_all, layer=j, n=D_MODEL, kind="resid", resid=h)


def kernel(x, mla_norm_g, mla_w_in, mla_q_a_g, mla_w_qb, mla_kv_a_g, mla_w_kvb, mla_q_norm_g,
           mla_k_norm_g, mla_w_out, sb_norm_g, sb_w_in, sb_w_out):
    cos_t, sin_t = rope_tables_fm()
    h = x.reshape(TOKENS, D_MODEL)
    for i in range(DEPTH):
        j = i // 2
        if i % 2 == 0:
            h = _mla_layer(h, j, mla_norm_g[j], mla_w_in[j], mla_q_a_g[j], mla_w_qb[j], mla_kv_a_g[j],
                           mla_w_kvb[j], mla_q_norm_g[j], mla_k_norm_g[j], mla_w_out, cos_t, sin_t)
        else:
            h = _sb_layer(h, j, sb_norm_g[j], sb_w_in, sb_w_out)
    return h.reshape(BATCH, SEQ, D_MODEL)
```

```python
import functools

import numpy as np
import jax
import jax.numpy as jnp
from jax import lax
from jax.experimental import pallas as pl
from jax.experimental.pallas import tpu as pltpu

D_MODEL = 4096
BATCH = 4
SEQ = 2048
DEPTH = 4
MLA_HEADS = 32
Q_LORA = 1024
KV_LORA = 512
NOPE_DIM = 128
ROPE_DIM = 64
V_DIM = 128
QK_DIM = NOPE_DIM + ROPE_DIM
MLA_WIDTH = MLA_HEADS * V_DIM
ROPE_THETA = 10000.0
SB_HEADS = 32
SB_DIM = 128
SB_WIDTH = SB_HEADS * SB_DIM
EPS = 1e-6
TOKENS = BATCH * SEQ

V7X_SUBLANES = 8
V7X_LANES = 128
V7X_MXU_DIM = 256
V7X_VMEM_BYTES = 64 * 1024 * 1024
VMEM_LIMIT_CAP = V7X_VMEM_BYTES - 8 * 1024 * 1024

QK_PAD = V7X_MXU_DIM
ATT_BLOCK = 512
N_ATT_BLOCKS = SEQ // ATT_BLOCK

BF16 = jnp.bfloat16
F32 = jnp.float32
NT_DIMS = (((1,), (1,)), ((), ()))


def _vmem_limit(*block_bytes, scratch=0):
    need = 2 * sum(block_bytes) + scratch
    return int(min(VMEM_LIMIT_CAP, need + 8 * 1024 * 1024))


def _nbytes(shape, dtype):
    return int(np.prod(shape)) * jnp.dtype(dtype).itemsize


def _norm_kernel(x_ref, g_ref, o_ref):
    x = x_ref[...]
    ms = jnp.mean(x * x, axis=-1, keepdims=True)
    o_ref[...] = (x * lax.rsqrt(ms + EPS) * g_ref[...]).astype(BF16)


def _norm_perm_kernel(x_ref, g_ref, perm_ref, o_ref, operm_ref):
    x = x_ref[...]
    ms = jnp.mean(x * x, axis=-1, keepdims=True)
    y = (x * lax.rsqrt(ms + EPS) * g_ref[...]).astype(BF16)
    o_ref[...] = y
    operm_ref[...] = jnp.dot(perm_ref[...], y, preferred_element_type=F32).astype(BF16)


def _scan_perm_matrix():
    r = np.arange(ATT_BLOCK)
    src = (r % V7X_SUBLANES) * (ATT_BLOCK // V7X_SUBLANES) + r // V7X_SUBLANES
    p = np.zeros((ATT_BLOCK, ATT_BLOCK), np.float32)
    p[r, src] = 1.0
    return jnp.asarray(p, BF16)


def rms_norm_tm(x, g, *, with_perm=False):
    bt = ATT_BLOCK if with_perm else 256
    grid = (TOKENS // bt,)
    x_spec = pl.BlockSpec((bt, D_MODEL), lambda i: (i, 0))
    g_spec = pl.BlockSpec((1, D_MODEL), lambda i: (0, 0))
    o_spec = pl.BlockSpec((bt, D_MODEL), lambda i: (i, 0))
    o_shape = jax.ShapeDtypeStruct((TOKENS, D_MODEL), BF16)
    xb = _nbytes((bt, D_MODEL), F32)
    ob = _nbytes((bt, D_MODEL), BF16)
    g2 = g.reshape(1, D_MODEL)
    if not with_perm:
        return pl.pallas_call(
            _norm_kernel, grid=grid, in_specs=[x_spec, g_spec], out_specs=o_spec,
            out_shape=o_shape, name="rms_norm",
            compiler_params=pltpu.CompilerParams(vmem_limit_bytes=_vmem_limit(xb, ob, scratch=2 * xb)),
        )(x, g2)
    p_spec = pl.BlockSpec((bt, bt), lambda i: (0, 0))
    return pl.pallas_call(
        _norm_perm_kernel, grid=grid, in_specs=[x_spec, g_spec, p_spec],
        out_specs=[o_spec, o_spec], out_shape=[o_shape, o_shape], name="rms_norm_perm",
        compiler_params=pltpu.CompilerParams(vmem_limit_bytes=_vmem_limit(xb, ob, ob, scratch=3 * xb)),
    )(x, g2, _scan_perm_matrix())


def _proj_kernel(*refs, kind, scale, cast):
    if kind == "resid":
        a_ref, w_ref, r_ref, o_ref = refs[:4]
        scratch = refs[4:]
    else:
        a_ref, w_ref, o_ref = refs[:3]
        scratch = refs[3:]
    if cast:
        (w_bf_ref,) = scratch

        @pl.when(pl.program_id(1) == 0)
        def _cast_weights():
            w_bf_ref[...] = w_ref[...].astype(BF16)

        w = w_bf_ref[...]
    else:
        w = w_ref[...]
    acc = jnp.dot(a_ref[...], w, preferred_element_type=F32)
    if kind == "resid":
        o_ref[...] = r_ref[...] + acc
    elif kind == "silu":
        o_ref[...] = (acc * jax.nn.sigmoid(acc)).astype(BF16)
    elif kind == "plain":
        o_ref[...] = acc.astype(BF16)
    else:
        assert kind == "fm"
        o_ref[...] = (acc * scale).T.astype(BF16) if scale != 1.0 else acc.T.astype(BF16)


def project(a, w, *, kind, layer=None, col0=0, n=None, scale=1.0, resid=None, bm=1024, bn=512):
    m, k = a.shape
    cast = w.ndim == 3
    if cast:
        assert col0 % bn == 0
        w_spec = pl.BlockSpec((None, k, bn), lambda j, i: (layer, 0, col0 // bn + j))
    else:
        n = w.shape[1]
        w_spec = pl.BlockSpec((k, bn), lambda j, i: (0, j))
    in_specs = [pl.BlockSpec((bm, k), lambda j, i: (i, 0)), w_spec]
    args = [a, w]
    out_bytes = _nbytes((bm, bn), BF16)
    aliases = {}
    if kind == "resid":
        in_specs.append(pl.BlockSpec((bm, bn), lambda j, i: (i, j)))
        args.append(resid)
        out_spec = pl.BlockSpec((bm, bn), lambda j, i: (i, j))
        out_shape = jax.ShapeDtypeStruct((m, n), F32)
        out_bytes = 2 * _nbytes((bm, bn), F32)
        aliases = {2: 0}
    elif kind == "fm":
        out_spec = pl.BlockSpec((bn, bm), lambda j, i: (j, i))
        out_shape = jax.ShapeDtypeStruct((n, m), BF16)
    else:
        out_spec = pl.BlockSpec((bm, bn), lambda j, i: (i, j))
        out_shape = jax.ShapeDtypeStruct((m, n), BF16)
    return pl.pallas_call(
        functools.partial(_proj_kernel, kind=kind, scale=scale, cast=cast),
        grid=(n // bn, m // bm), in_specs=in_specs, out_specs=out_spec, out_shape=out_shape,
        scratch_shapes=[pltpu.VMEM((k, bn), BF16)] if cast else [],
        input_output_aliases=aliases, name="proj_" + kind,
        compiler_params=pltpu.CompilerParams(vmem_limit_bytes=_vmem_limit(
            _nbytes((bm, k), BF16), _nbytes((k, bn), F32 if cast else BF16), out_bytes,
            scratch=_nbytes((k, bn), BF16) + 3 * _nbytes((bm, bn), F32))),
    )(*args)


def _rope_kernel(inv_ref, cos_ref, sin_ref):
    pos = lax.broadcasted_iota(jnp.int32, cos_ref.shape, 1).astype(F32)
    ang = pos * inv_ref[...]
    cos_ref[...] = jnp.cos(ang)
    sin_ref[...] = jnp.sin(ang)


def rope_tables_fm():
    half = ROPE_DIM // 2
    inv = 1.0 / (np.float32(ROPE_THETA) ** (np.arange(0, ROPE_DIM, 2, dtype=np.float32) / np.float32(ROPE_DIM)))
    inv = jnp.asarray(np.broadcast_to(inv.astype(np.float32)[:, None], (half, SEQ)))
    shape = jax.ShapeDtypeStruct((half, SEQ), F32)
    return pl.pallas_call(_rope_kernel, out_shape=[shape, shape], name="rope_tables")(inv)


def _rope_fm(x, cos, sin):
    half = ROPE_DIM // 2
    x1, x2 = x[:half], x[half:]
    return jnp.concatenate([x1 * cos - x2 * sin, x1 * sin + x2 * cos], axis=0)


def _mla_latent_kernel(hn_ref, wlat_ref, wpe_t_ref, qg_ref, kvg_ref, qn_ref, kvn_ref, kpe_t_ref):
    hn = hn_ref[...]
    acc = jnp.dot(hn, wlat_ref[...], preferred_element_type=F32)
    ql = acc[:, :Q_LORA]
    kvl = acc[:, Q_LORA:]
    qn_ref[...] = (ql * lax.rsqrt(jnp.mean(ql * ql, axis=-1, keepdims=True) + EPS) * qg_ref[...]).astype(BF16)
    kvn_ref[...] = (kvl * lax.rsqrt(jnp.mean(kvl * kvl, axis=-1, keepdims=True) + EPS) * kvg_ref[...]).astype(BF16)
    kpe_t_ref[...] = lax.dot_general(wpe_t_ref[...], hn, NT_DIMS, preferred_element_type=F32)


def mla_latents(hn, w_lat, w_pe_t, q_a_g, kv_a_g, *, bt=512):
    nlat = Q_LORA + KV_LORA
    return pl.pallas_call(
        _mla_latent_kernel, grid=(TOKENS // bt,),
        in_specs=[pl.BlockSpec((bt, D_MODEL), lambda i: (i, 0)),
                  pl.BlockSpec((D_MODEL, nlat), lambda i: (0, 0)),
                  pl.BlockSpec((ROPE_DIM, D_MODEL), lambda i: (0, 0)),
                  pl.BlockSpec((1, Q_LORA), lambda i: (0, 0)),
                  pl.BlockSpec((1, KV_LORA), lambda i: (0, 0))],
        out_specs=[pl.BlockSpec((bt, Q_LORA), lambda i: (i, 0)),
                   pl.BlockSpec((bt, KV_LORA), lambda i: (i, 0)),
                   pl.BlockSpec((ROPE_DIM, bt), lambda i: (0, i))],
        out_shape=[jax.ShapeDtypeStruct((TOKENS, Q_LORA), BF16),
                   jax.ShapeDtypeStruct((TOKENS, KV_LORA), BF16),
                   jax.ShapeDtypeStruct((ROPE_DIM, TOKENS), F32)],
        name="mla_latents",
        compiler_params=pltpu.CompilerParams(vmem_limit_bytes=_vmem_limit(
            _nbytes((bt, D_MODEL), BF16), _nbytes((D_MODEL, nlat), BF16), _nbytes((ROPE_DIM, D_MODEL), BF16),
            _nbytes((bt, nlat), BF16), _nbytes((ROPE_DIM, bt), F32), scratch=2 * _nbytes((bt, nlat), F32))),
    )(hn, w_lat, w_pe_t, q_a_g.reshape(1, Q_LORA), kv_a_g.reshape(1, KV_LORA))


def _mla_q_kernel(w_ref, qn_ref, g_ref, cos_ref, sin_ref, q_t_ref, *, heads, scale):
    acc = lax.dot_general(w_ref[...], qn_ref[...], NT_DIMS, preferred_element_type=F32)
    g = g_ref[...]
    cos = cos_ref[...]
    sin = sin_ref[...]
    bt = acc.shape[1]
    pad = jnp.zeros((QK_PAD - QK_DIM, bt), F32)
    for h in range(heads):
        q = acc[h * QK_DIM:(h + 1) * QK_DIM]
        r = lax.rsqrt(jnp.mean(q * q, axis=0, keepdims=True) + EPS)
        qn = q * r * g
        out = jnp.concatenate([qn[:NOPE_DIM], _rope_fm(qn[NOPE_DIM:], cos, sin), pad], axis=0)
        q_t_ref[h] = (out * scale).astype(BF16)


def mla_q_heads(w_qb_t, qn, q_norm_g, cos_t, sin_t, *, heads=8, bt=512):
    rows = heads * QK_DIM
    g = jnp.broadcast_to(q_norm_g[:, None], (QK_DIM, bt))
    nseq = SEQ // bt
    return pl.pallas_call(
        functools.partial(_mla_q_kernel, heads=heads, scale=QK_DIM ** -0.5 * np.log2(np.e)),
        grid=(TOKENS // bt, MLA_HEADS // heads),
        in_specs=[pl.BlockSpec((rows, Q_LORA), lambda i, j: (j, 0)),
                  pl.BlockSpec((bt, Q_LORA), lambda i, j: (i, 0)),
                  pl.BlockSpec((QK_DIM, bt), lambda i, j: (0, 0)),
                  pl.BlockSpec((ROPE_DIM // 2, bt), lambda i, j: (0, i % nseq)),
                  pl.BlockSpec((ROPE_DIM // 2, bt), lambda i, j: (0, i % nseq))],
        out_specs=pl.BlockSpec((heads, QK_PAD, bt), lambda i, j: (j, 0, i)),
        out_shape=jax.ShapeDtypeStruct((MLA_HEADS, QK_PAD, TOKENS), BF16), name="mla_q_heads",
        compiler_params=pltpu.CompilerParams(vmem_limit_bytes=_vmem_limit(
            _nbytes((rows, Q_LORA), BF16), _nbytes((bt, Q_LORA), BF16), _nbytes((QK_DIM, bt), F32),
            _nbytes((heads, QK_PAD, bt), BF16), scratch=3 * _nbytes((rows, bt), F32))),
    )(w_qb_t, qn, g, cos_t, sin_t)


def _mla_kv_kernel(w_ref, kvn_ref, kpe_t_ref, g_ref, cos_ref, sin_ref, k_ref, v_t_ref, *, heads):
    acc = lax.dot_general(w_ref[...], kvn_ref[...], NT_DIMS, preferred_element_type=F32)
    g = g_ref[...]
    bt = acc.shape[1]
    kpe = kpe_t_ref[...]
    ss_pe = jnp.sum(kpe * kpe, axis=0, keepdims=True)
    kpe_rot = _rope_fm(kpe * g[NOPE_DIM:], cos_ref[...], sin_ref[...])
    pad = jnp.zeros((QK_PAD - QK_DIM, bt), F32)
    per_head = NOPE_DIM + V_DIM
    for h in range(heads):
        kn = acc[h * per_head:h * per_head + NOPE_DIM]
        v = acc[h * per_head + NOPE_DIM:(h + 1) * per_head]
        ss = jnp.sum(kn * kn, axis=0, keepdims=True) + ss_pe
        r = lax.rsqrt(ss * (1.0 / QK_DIM) + EPS)
        k_t = jnp.concatenate([kn * g[:NOPE_DIM] * r, kpe_rot * r, pad], axis=0)
        k_ref[h] = k_t.T.astype(BF16)
        v_t_ref[h] = v.astype(BF16)


def mla_kv_heads(w_kvb_t, kvn, kpe_t, k_norm_g, cos_t, sin_t, *, heads=4):
    bt = ATT_BLOCK
    rows = heads * (NOPE_DIM + V_DIM)
    g = jnp.broadcast_to(k_norm_g[:, None], (QK_DIM, bt))
    nseq = SEQ // bt
    return pl.pallas_call(
        functools.partial(_mla_kv_kernel, heads=heads),
        grid=(TOKENS // bt, MLA_HEADS // heads),
        in_specs=[pl.BlockSpec((rows, KV_LORA), lambda i, j: (j, 0)),
                  pl.BlockSpec((bt, KV_LORA), lambda i, j: (i, 0)),
                  pl.BlockSpec((ROPE_DIM, bt), lambda i, j: (0, i)),
                  pl.BlockSpec((QK_DIM, bt), lambda i, j: (0, 0)),
                  pl.BlockSpec((ROPE_DIM // 2, bt), lambda i, j: (0, i % nseq)),
                  pl.BlockSpec((ROPE_DIM // 2, bt), lambda i, j: (0, i % nseq))],
        out_specs=[pl.BlockSpec((heads, bt, QK_PAD), lambda i, j: (j, i, 0)),
                   pl.BlockSpec((heads, V_DIM, bt), lambda i, j: (j, 0, i))],
        out_shape=[jax.ShapeDtypeStruct((MLA_HEADS, TOKENS, QK_PAD), BF16),
                   jax.ShapeDtypeStruct((MLA_HEADS, V_DIM, TOKENS), BF16)],
        name="mla_kv_heads",
        compiler_params=pltpu.CompilerParams(vmem_limit_bytes=_vmem_limit(
            _nbytes((rows, KV_LORA), BF16), _nbytes((bt, KV_LORA), BF16), _nbytes((ROPE_DIM, bt), F32),
            _nbytes((QK_DIM, bt), F32), _nbytes((heads, bt, QK_PAD), BF16), _nbytes((heads, V_DIM, bt), BF16),
            scratch=3 * _nbytes((rows, bt), F32))),
    )(w_kvb_t, kvn, kpe_t, g, cos_t, sin_t)


def _gated_store(y_ref, qi, o_t, sg_ref):
    rows = slice(qi * ATT_BLOCK, (qi + 1) * ATT_BLOCK)
    y_ref[rows, :] = (o_t.T * sg_ref[rows, :].astype(F32)).astype(BF16)


ROW_CHUNK = 64


def _sublane_groups(x):
    return x.reshape(x.shape[0] // V7X_SUBLANES, V7X_SUBLANES, x.shape[1])


def _mla_attn_kernel(q_t_ref, k_ref, v_t_ref, sg_ref, y_ref, *p_refs):
    blk = ATT_BLOCK
    causal = (lax.broadcasted_iota(jnp.int32, (blk, blk), 0)
              <= lax.broadcasted_iota(jnp.int32, (blk, blk), 1))
    for qi in range(N_ATT_BLOCKS):
        p_ref = p_refs[qi]
        lo, nk = qi * blk, (qi + 1) * blk
        q_t = q_t_ref[:, lo:nk]
        s = jnp.dot(k_ref[0:nk, :], q_t, preferred_element_type=F32)
        s_diag = jnp.where(causal, s[lo:nk], -jnp.inf)
        m8 = jnp.max(_sublane_groups(s_diag), axis=0)
        if qi > 0:
            m8 = jnp.maximum(m8, jnp.max(_sublane_groups(s[0:lo]), axis=0))
        m = jnp.max(m8, axis=0, keepdims=True)
        l8 = jnp.zeros((V7X_SUBLANES, blk), F32)
        for r in range(0, nk, ROW_CHUNK):
            rows = s_diag[r - lo:r - lo + ROW_CHUNK] if r >= lo else s[r:r + ROW_CHUNK]
            p = jnp.exp2(rows - m)
            l8 = l8 + jnp.sum(_sublane_groups(p), axis=0)
            p_ref[r:r + ROW_CHUNK, :] = p.astype(BF16)
        l = jnp.sum(l8, axis=0, keepdims=True)
        pv = jnp.dot(v_t_ref[:, 0:nk], p_ref[...], preferred_element_type=F32)
        _gated_store(y_ref, qi, pv / l, sg_ref)


def mla_attention(q_t, k, v_t, sg):
    blk = ATT_BLOCK
    p_scratch = [pltpu.VMEM(((qi + 1) * blk, blk), BF16) for qi in range(N_ATT_BLOCKS)]
    return pl.pallas_call(
        _mla_attn_kernel, grid=(BATCH, MLA_HEADS),
        in_specs=[pl.BlockSpec((None, QK_PAD, SEQ), lambda b, h: (h, 0, b)),
                  pl.BlockSpec((None, SEQ, QK_PAD), lambda b, h: (h, b, 0)),
                  pl.BlockSpec((None, V_DIM, SEQ), lambda b, h: (h, 0, b)),
                  pl.BlockSpec((SEQ, V_DIM), lambda b, h: (b, h))],
        out_specs=pl.BlockSpec((SEQ, V_DIM), lambda b, h: (b, h)),
        out_shape=jax.ShapeDtypeStruct((TOKENS, MLA_WIDTH), BF16),
        scratch_shapes=p_scratch, name="mla_attention",
        compiler_params=pltpu.CompilerParams(vmem_limit_bytes=_vmem_limit(
            _nbytes((QK_PAD, SEQ), BF16), _nbytes((SEQ, QK_PAD), BF16), _nbytes((V_DIM, SEQ), BF16),
            2 * _nbytes((SEQ, V_DIM), BF16), scratch=30 * _nbytes((blk, blk), F32))),
    )(q_t, k, v_t, sg)


def _sublane_suffix_products(tot):
    sub = lax.broadcasted_iota(jnp.int32, tot.shape, 0)
    one = jnp.ones_like(tot)

    def shifted(x, k):
        return jnp.where(sub < V7X_SUBLANES - k, pltpu.roll(x, V7X_SUBLANES - k, axis=0), one)

    e = shifted(tot, 1)
    e = e * shifted(e, 1)
    e = e * shifted(e, 2)
    e = e * shifted(e, 4)
    full = tot * e
    return e, jnp.broadcast_to(full[0:1], tot.shape)


def _product(xs):
    xs = list(xs)
    while len(xs) > 1:
        xs = [xs[i] * xs[i + 1] for i in range(0, len(xs), 2)]
    return xs[0]


def _sb_attn_kernel(q_t_ref, k_ref, v_t_ref, sg_ref, y_ref, *scratch):
    blk = ATT_BLOCK
    groups = blk // V7X_SUBLANES
    per_chunk = ROW_CHUNK // V7X_SUBLANES
    chunks = blk // ROW_CHUNK
    g_refs, w_refs = scratch[:N_ATT_BLOCKS], scratch[N_ATT_BLOCKS:]
    slack = (lax.broadcasted_iota(jnp.int32, (V7X_SUBLANES, blk), 1)
             - groups * lax.broadcasted_iota(jnp.int32, (V7X_SUBLANES, blk), 0))
    for qi in range(N_ATT_BLOCKS):
        g_ref, w_ref = g_refs[qi], w_refs[qi]
        q_t = q_t_ref[:, qi * blk:(qi + 1) * blk]
        carry = None
        for kj in range(qi, -1, -1):
            zh = jnp.dot(k_ref[kj * blk:(kj + 1) * blk, :], q_t, preferred_element_type=F32)
            chunk_total = []
            for c in range(chunks):
                rows = slice(c * ROW_CHUNK, (c + 1) * ROW_CHUNK)
                g3 = _sublane_groups(0.5 - 0.5 * jnp.tanh(zh[rows]))
                gs = [g3[i] for i in range(per_chunk)]
                if kj == qi:
                    gs = [jnp.where(slack > c * per_chunk + i, gs[i], 1.0) for i in range(per_chunk)]
                g_ref[rows, :] = jnp.concatenate(gs, axis=0)
                chunk_total.append(_product(gs))
            later_sublanes, block_total = _sublane_suffix_products(_product(chunk_total))
            run = later_sublanes if carry is None else carry * later_sublanes
            chunk_run = [None] * chunks
            for c in reversed(range(chunks)):
                chunk_run[c] = run
                run = run * chunk_total[c]
            for c in range(chunks):
                g3 = _sublane_groups(g_ref[c * ROW_CHUNK:(c + 1) * ROW_CHUNK, :])
                run = chunk_run[c]
                w = [None] * per_chunk
                for i in reversed(range(per_chunk)):
                    before = run * g3[i]
                    w[i] = run - before
                    run = before
                r0 = kj * blk + c * ROW_CHUNK
                w_ref[r0:r0 + ROW_CHUNK, :] = jnp.concatenate(w, axis=0).astype(BF16)
            carry = block_total if carry is None else carry * block_total
        pv = jnp.dot(v_t_ref[:, 0:(qi + 1) * blk], w_ref[...], preferred_element_type=F32)
        _gated_store(y_ref, qi, pv, sg_ref)


def sb_attention(q_t, k, v_t, sg):
    blk = ATT_BLOCK
    scratch = ([pltpu.VMEM((blk, blk), F32) for _ in range(N_ATT_BLOCKS)]
               + [pltpu.VMEM(((qi + 1) * blk, blk), BF16) for qi in range(N_ATT_BLOCKS)])
    return pl.pallas_call(
        _sb_attn_kernel, grid=(BATCH, SB_HEADS),
        in_specs=[pl.BlockSpec((SB_DIM, SEQ), lambda b, h: (h, b)),
                  pl.BlockSpec((SEQ, SB_DIM), lambda b, h: (b, h)),
                  pl.BlockSpec((SB_DIM, SEQ), lambda b, h: (h, b)),
                  pl.BlockSpec((SEQ, SB_DIM), lambda b, h: (b, h))],
        out_specs=pl.BlockSpec((SEQ, SB_DIM), lambda b, h: (b, h)),
        out_shape=jax.ShapeDtypeStruct((TOKENS, SB_WIDTH), BF16),
        scratch_shapes=scratch, name="sb_attention",
        compiler_params=pltpu.CompilerParams(vmem_limit_bytes=_vmem_limit(
            _nbytes((SB_DIM, SEQ), BF16), _nbytes((SEQ, SB_DIM), BF16), _nbytes((SB_DIM, SEQ), BF16),
            2 * _nbytes((SEQ, SB_DIM), BF16), scratch=30 * _nbytes((blk, blk), F32))),
    )(q_t, k, v_t, sg)


def _mla_layer(h, j, norm_g, w_in, q_a_g, w_qb, kv_a_g, w_kvb, q_norm_g, k_norm_g, w_out_all, cos_t, sin_t):
    nlat = Q_LORA + KV_LORA
    w_lat = w_in[:, :nlat].astype(BF16)
    w_pe_t = w_in[:, nlat:nlat + ROPE_DIM].T.astype(BF16)
    w_gate = w_in[:, nlat + ROPE_DIM:].astype(BF16)
    hn = rms_norm_tm(h, norm_g)
    sg = project(hn, w_gate, kind="silu")
    qn, kvn, kpe_t = mla_latents(hn, w_lat, w_pe_t, q_a_g, kv_a_g)
    q_t = mla_q_heads(w_qb.T.astype(BF16), qn, q_norm_g, cos_t, sin_t)
    k, v_t = mla_kv_heads(w_kvb.T.astype(BF16), kvn, kpe_t, k_norm_g, cos_t, sin_t)
    y = mla_attention(q_t, k, v_t, sg)
    return project(y, w_out_all, layer=j, n=D_MODEL, kind="resid", resid=h)


def _sb_layer(h, j, norm_g, w_in_all, w_out_all):
    w = SB_WIDTH
    hn, hn_perm = rms_norm_tm(h, norm_g, with_perm=True)
    q_t = project(hn, w_in_all, layer=j, col0=0, n=w, kind="fm", scale=0.5 * SB_DIM ** -0.5)
    k = project(hn_perm, w_in_all, layer=j, col0=w, n=w, kind="plain")
    v_t = project(hn_perm, w_in_all, layer=j, col0=2 * w, n=w, kind="fm")
    sg = project(hn, w_in_all, layer=j, col0=3 * w, n=w, kind="silu")
    y = sb_attention(q_t, k, v_t, sg)
    return project(y, w_out_all, layer=j, n=D_MODEL, kind="resid", resid=h)


def kernel(x, mla_norm_g, mla_w_in, mla_q_a_g, mla_w_qb, mla_kv_a_g, mla_w_kvb, mla_q_norm_g,
           mla_k_norm_g, mla_w_out, sb_norm_g, sb_w_in, sb_w_out):
    cos_t, sin_t = rope_tables_fm()
    h = x.reshape(TOKENS, D_MODEL)
    for i in range(DEPTH):
        j = i // 2
        if i % 2 == 0:
            h = _mla_layer(h, j, mla_norm_g[j], mla_w_in[j], mla_q_a_g[j], mla_w_qb[j], mla_kv_a_g[j],
                           mla_w_kvb[j], mla_q_norm_g[j], mla_k_norm_g[j], mla_w_out, cos_t, sin_t)
        else:
            h = _sb_layer(h, j, sb_norm_g[j], sb_w_in, sb_w_out)
    return h.reshape(BATCH, SEQ, D_MODEL)
```

```python
import functools

import numpy as np
import jax
import jax.numpy as jnp
from jax import lax
from jax.experimental import pallas as pl
from jax.experimental.pallas import tpu as pltpu

D_MODEL = 4096
BATCH = 4
SEQ = 2048
DEPTH = 4
MLA_HEADS = 32
Q_LORA = 1024
KV_LORA = 512
NOPE_DIM = 128
ROPE_DIM = 64
V_DIM = 128
QK_DIM = NOPE_DIM + ROPE_DIM
MLA_WIDTH = MLA_HEADS * V_DIM
ROPE_THETA = 10000.0
SB_HEADS = 32
SB_DIM = 128
SB_WIDTH = SB_HEADS * SB_DIM
EPS = 1e-6
TOKENS = BATCH * SEQ

V7X_SUBLANES = 8
V7X_LANES = 128
V7X_MXU_DIM = 256
V7X_VMEM_BYTES = 64 * 1024 * 1024
VMEM_LIMIT_CAP = V7X_VMEM_BYTES - 8 * 1024 * 1024

QK_PAD = V7X_MXU_DIM
V_ROWS = V_DIM + 2 * V7X_SUBLANES
ATT_BLOCK = 512
N_ATT_BLOCKS = SEQ // ATT_BLOCK

BF16 = jnp.bfloat16
F32 = jnp.float32
NT_DIMS = (((1,), (1,)), ((), ()))


def _vmem_limit(*block_bytes, scratch=0):
    need = 2 * sum(block_bytes) + scratch
    return int(min(VMEM_LIMIT_CAP, need + 8 * 1024 * 1024))


def _nbytes(shape, dtype):
    return int(np.prod(shape)) * jnp.dtype(dtype).itemsize


def _norm_kernel(x_ref, g_ref, o_ref):
    x = x_ref[...]
    ms = jnp.mean(x * x, axis=-1, keepdims=True)
    o_ref[...] = (x * lax.rsqrt(ms + EPS) * g_ref[...]).astype(BF16)


def _norm_perm_kernel(x_ref, g_ref, perm_ref, o_ref, operm_ref):
    x = x_ref[...]
    ms = jnp.mean(x * x, axis=-1, keepdims=True)
    y = (x * lax.rsqrt(ms + EPS) * g_ref[...]).astype(BF16)
    o_ref[...] = y
    operm_ref[...] = jnp.dot(perm_ref[...], y, preferred_element_type=F32).astype(BF16)


def _scan_perm_matrix():
    r = np.arange(ATT_BLOCK)
    src = (r % V7X_SUBLANES) * (ATT_BLOCK // V7X_SUBLANES) + r // V7X_SUBLANES
    p = np.zeros((ATT_BLOCK, ATT_BLOCK), np.float32)
    p[r, src] = 1.0
    return jnp.asarray(p, BF16)


def rms_norm_tm(x, g, *, with_perm=False):
    bt = ATT_BLOCK if with_perm else 256
    grid = (TOKENS // bt,)
    x_spec = pl.BlockSpec((bt, D_MODEL), lambda i: (i, 0))
    g_spec = pl.BlockSpec((1, D_MODEL), lambda i: (0, 0))
    o_spec = pl.BlockSpec((bt, D_MODEL), lambda i: (i, 0))
    o_shape = jax.ShapeDtypeStruct((TOKENS, D_MODEL), BF16)
    xb = _nbytes((bt, D_MODEL), F32)
    ob = _nbytes((bt, D_MODEL), BF16)
    g2 = g.reshape(1, D_MODEL)
    if not with_perm:
        return pl.pallas_call(
            _norm_kernel, grid=grid, in_specs=[x_spec, g_spec], out_specs=o_spec,
            out_shape=o_shape, name="rms_norm",
            compiler_params=pltpu.CompilerParams(vmem_limit_bytes=_vmem_limit(xb, ob, scratch=2 * xb)),
        )(x, g2)
    p_spec = pl.BlockSpec((bt, bt), lambda i: (0, 0))
    return pl.pallas_call(
        _norm_perm_kernel, grid=grid, in_specs=[x_spec, g_spec, p_spec],
        out_specs=[o_spec, o_spec], out_shape=[o_shape, o_shape], name="rms_norm_perm",
        compiler_params=pltpu.CompilerParams(vmem_limit_bytes=_vmem_limit(xb, ob, ob, scratch=3 * xb)),
    )(x, g2, _scan_perm_matrix())


def _proj_kernel(*refs, kind, scale, cast):
    if kind == "resid":
        a_ref, w_ref, r_ref, o_ref = refs[:4]
        scratch = refs[4:]
    else:
        a_ref, w_ref, o_ref = refs[:3]
        scratch = refs[3:]
    if cast:
        (w_bf_ref,) = scratch

        @pl.when(pl.program_id(1) == 0)
        def _cast_weights():
            w_bf_ref[...] = w_ref[...].astype(BF16)

        w = w_bf_ref[...]
    else:
        w = w_ref[...]
    acc = jnp.dot(a_ref[...], w, preferred_element_type=F32)
    if kind == "resid":
        o_ref[...] = r_ref[...] + acc
    elif kind == "silu":
        o_ref[...] = (acc * jax.nn.sigmoid(acc)).astype(BF16)
    elif kind == "plain":
        o_ref[...] = acc.astype(BF16)
    else:
        assert kind == "fm"
        o_ref[...] = (acc * scale).T.astype(BF16) if scale != 1.0 else acc.T.astype(BF16)


def project(a, w, *, kind, layer=None, col0=0, n=None, scale=1.0, resid=None, bm=1024, bn=512):
    m, k = a.shape
    cast = w.ndim == 3
    if cast:
        assert col0 % bn == 0
        w_spec = pl.BlockSpec((None, k, bn), lambda j, i: (layer, 0, col0 // bn + j))
    else:
        n = w.shape[1]
        w_spec = pl.BlockSpec((k, bn), lambda j, i: (0, j))
    in_specs = [pl.BlockSpec((bm, k), lambda j, i: (i, 0)), w_spec]
    args = [a, w]
    out_bytes = _nbytes((bm, bn), BF16)
    if kind == "resid":
        in_specs.append(pl.BlockSpec((bm, bn), lambda j, i: (i, j)))
        args.append(resid)
        out_spec = pl.BlockSpec((bm, bn), lambda j, i: (i, j))
        out_shape = jax.ShapeDtypeStruct((m, n), F32)
        out_bytes = 2 * _nbytes((bm, bn), F32)
    elif kind == "fm":
        out_spec = pl.BlockSpec((bn, bm), lambda j, i: (j, i))
        out_shape = jax.ShapeDtypeStruct((n, m), BF16)
    else:
        out_spec = pl.BlockSpec((bm, bn), lambda j, i: (i, j))
        out_shape = jax.ShapeDtypeStruct((m, n), BF16)
    return pl.pallas_call(
        functools.partial(_proj_kernel, kind=kind, scale=scale, cast=cast),
        grid=(n // bn, m // bm), in_specs=in_specs, out_specs=out_spec, out_shape=out_shape,
        scratch_shapes=[pltpu.VMEM((k, bn), BF16)] if cast else [],
        name="proj_" + kind,
        compiler_params=pltpu.CompilerParams(vmem_limit_bytes=_vmem_limit(
            _nbytes((bm, k), BF16), _nbytes((k, bn), F32 if cast else BF16), out_bytes,
            scratch=_nbytes((k, bn), BF16) + 3 * _nbytes((bm, bn), F32))),
    )(*args)


def _rope_kernel(inv_ref, cos_ref, sin_ref):
    pos = lax.broadcasted_iota(jnp.int32, cos_ref.shape, 1).astype(F32)
    ang = pos * inv_ref[...]
    cos_ref[...] = jnp.cos(ang)
    sin_ref[...] = jnp.sin(ang)


def rope_tables_fm():
    half = ROPE_DIM // 2
    inv = 1.0 / (np.float32(ROPE_THETA) ** (np.arange(0, ROPE_DIM, 2, dtype=np.float32) / np.float32(ROPE_DIM)))
    inv = jnp.asarray(np.broadcast_to(inv.astype(np.float32)[:, None], (half, SEQ)))
    shape = jax.ShapeDtypeStruct((half, SEQ), F32)
    return pl.pallas_call(_rope_kernel, out_shape=[shape, shape], name="rope_tables")(inv)


def _rope_fm(x, cos, sin):
    half = ROPE_DIM // 2
    x1, x2 = x[:half], x[half:]
    return jnp.concatenate([x1 * cos - x2 * sin, x1 * sin + x2 * cos], axis=0)


def _mla_latent_kernel(hn_ref, wlat_ref, wpe_t_ref, qg_ref, kvg_ref, qn_ref, kvn_ref, kpe_t_ref):
    hn = hn_ref[...]
    acc = jnp.dot(hn, wlat_ref[...], preferred_element_type=F32)
    ql = acc[:, :Q_LORA]
    kvl = acc[:, Q_LORA:]
    qn_ref[...] = (ql * lax.rsqrt(jnp.mean(ql * ql, axis=-1, keepdims=True) + EPS) * qg_ref[...]).astype(BF16)
    kvn_ref[...] = (kvl * lax.rsqrt(jnp.mean(kvl * kvl, axis=-1, keepdims=True) + EPS) * kvg_ref[...]).astype(BF16)
    kpe_t_ref[...] = lax.dot_general(wpe_t_ref[...], hn, NT_DIMS, preferred_element_type=F32)


def mla_latents(hn, w_lat, w_pe_t, q_a_g, kv_a_g, *, bt=512):
    nlat = Q_LORA + KV_LORA
    return pl.pallas_call(
        _mla_latent_kernel, grid=(TOKENS // bt,),
        in_specs=[pl.BlockSpec((bt, D_MODEL), lambda i: (i, 0)),
                  pl.BlockSpec((D_MODEL, nlat), lambda i: (0, 0)),
                  pl.BlockSpec((ROPE_DIM, D_MODEL), lambda i: (0, 0)),
                  pl.BlockSpec((1, Q_LORA), lambda i: (0, 0)),
                  pl.BlockSpec((1, KV_LORA), lambda i: (0, 0))],
        out_specs=[pl.BlockSpec((bt, Q_LORA), lambda i: (i, 0)),
                   pl.BlockSpec((bt, KV_LORA), lambda i: (i, 0)),
                   pl.BlockSpec((ROPE_DIM, bt), lambda i: (0, i))],
        out_shape=[jax.ShapeDtypeStruct((TOKENS, Q_LORA), BF16),
                   jax.ShapeDtypeStruct((TOKENS, KV_LORA), BF16),
                   jax.ShapeDtypeStruct((ROPE_DIM, TOKENS), F32)],
        name="mla_latents",
        compiler_params=pltpu.CompilerParams(vmem_limit_bytes=_vmem_limit(
            _nbytes((bt, D_MODEL), BF16), _nbytes((D_MODEL, nlat), BF16), _nbytes((ROPE_DIM, D_MODEL), BF16),
            _nbytes((bt, nlat), BF16), _nbytes((ROPE_DIM, bt), F32), scratch=2 * _nbytes((bt, nlat), F32))),
    )(hn, w_lat, w_pe_t, q_a_g.reshape(1, Q_LORA), kv_a_g.reshape(1, KV_LORA))


def _mla_q_kernel(w_ref, qn_ref, g_ref, cos_ref, sin_ref, q_t_ref, *, heads, scale):
    acc = lax.dot_general(w_ref[...], qn_ref[...], NT_DIMS, preferred_element_type=F32)
    g = g_ref[...]
    cos = cos_ref[...]
    sin = sin_ref[...]
    bt = acc.shape[1]
    pad = jnp.zeros((QK_PAD - QK_DIM, bt), F32)
    for h in range(heads):
        q = acc[h * QK_DIM:(h + 1) * QK_DIM]
        r = lax.rsqrt(jnp.mean(q * q, axis=0, keepdims=True) + EPS)
        qn = q * r * g
        out = jnp.concatenate([qn[:NOPE_DIM], _rope_fm(qn[NOPE_DIM:], cos, sin), pad], axis=0)
        q_t_ref[h] = (out * scale).astype(BF16)


def mla_q_heads(w_qb_t, qn, q_norm_g, cos_t, sin_t, *, heads=8, bt=512):
    rows = heads * QK_DIM
    g = jnp.broadcast_to(q_norm_g[:, None], (QK_DIM, bt))
    nseq = SEQ // bt
    return pl.pallas_call(
        functools.partial(_mla_q_kernel, heads=heads, scale=QK_DIM ** -0.5 * np.log2(np.e)),
        grid=(TOKENS // bt, MLA_HEADS // heads),
        in_specs=[pl.BlockSpec((rows, Q_LORA), lambda i, j: (j, 0)),
                  pl.BlockSpec((bt, Q_LORA), lambda i, j: (i, 0)),
                  pl.BlockSpec((QK_DIM, bt), lambda i, j: (0, 0)),
                  pl.BlockSpec((ROPE_DIM // 2, bt), lambda i, j: (0, i % nseq)),
                  pl.BlockSpec((ROPE_DIM // 2, bt), lambda i, j: (0, i % nseq))],
        out_specs=pl.BlockSpec((heads, QK_PAD, bt), lambda i, j: (j, 0, i)),
        out_shape=jax.ShapeDtypeStruct((MLA_HEADS, QK_PAD, TOKENS), BF16), name="mla_q_heads",
        compiler_params=pltpu.CompilerParams(vmem_limit_bytes=_vmem_limit(
            _nbytes((rows, Q_LORA), BF16), _nbytes((bt, Q_LORA), BF16), _nbytes((QK_DIM, bt), F32),
            _nbytes((heads, QK_PAD, bt), BF16), scratch=3 * _nbytes((rows, bt), F32))),
    )(w_qb_t, qn, g, cos_t, sin_t)


def _mla_kv_kernel(w_ref, kvn_ref, kpe_t_ref, g_ref, cos_ref, sin_ref, k_ref, v_t_ref, *, heads):
    acc = lax.dot_general(w_ref[...], kvn_ref[...], NT_DIMS, preferred_element_type=F32)
    g = g_ref[...]
    bt = acc.shape[1]
    kpe = kpe_t_ref[...]
    ss_pe = jnp.sum(kpe * kpe, axis=0, keepdims=True)
    kpe_rot = _rope_fm(kpe * g[NOPE_DIM:], cos_ref[...], sin_ref[...])
    pad = jnp.zeros((QK_PAD - QK_DIM, bt), F32)
    per_head = NOPE_DIM + V_DIM
    for h in range(heads):
        kn = acc[h * per_head:h * per_head + NOPE_DIM]
        v = acc[h * per_head + NOPE_DIM:(h + 1) * per_head]
        ss = jnp.sum(kn * kn, axis=0, keepdims=True) + ss_pe
        r = lax.rsqrt(ss * (1.0 / QK_DIM) + EPS)
        k_t = jnp.concatenate([kn * g[:NOPE_DIM] * r, kpe_rot * r, pad], axis=0)
        k_ref[h] = k_t.T.astype(BF16)
        v_t_ref[h, :V_DIM] = v.astype(BF16)
        v_t_ref[h, V_DIM:] = jnp.ones((V_ROWS - V_DIM, bt), BF16)


def mla_kv_heads(w_kvb_t, kvn, kpe_t, k_norm_g, cos_t, sin_t, *, heads=4):
    bt = ATT_BLOCK
    rows = heads * (NOPE_DIM + V_DIM)
    g = jnp.broadcast_to(k_norm_g[:, None], (QK_DIM, bt))
    nseq = SEQ // bt
    return pl.pallas_call(
        functools.partial(_mla_kv_kernel, heads=heads),
        grid=(TOKENS // bt, MLA_HEADS // heads),
        in_specs=[pl.BlockSpec((rows, KV_LORA), lambda i, j: (j, 0)),
                  pl.BlockSpec((bt, KV_LORA), lambda i, j: (i, 0)),
                  pl.BlockSpec((ROPE_DIM, bt), lambda i, j: (0, i)),
                  pl.BlockSpec((QK_DIM, bt), lambda i, j: (0, 0)),
                  pl.BlockSpec((ROPE_DIM // 2, bt), lambda i, j: (0, i % nseq)),
                  pl.BlockSpec((ROPE_DIM // 2, bt), lambda i, j: (0, i % nseq))],
        out_specs=[pl.BlockSpec((heads, bt, QK_PAD), lambda i, j: (j, i, 0)),
                   pl.BlockSpec((heads, V_ROWS, bt), lambda i, j: (j, 0, i))],
        out_shape=[jax.ShapeDtypeStruct((MLA_HEADS, TOKENS, QK_PAD), BF16),
                   jax.ShapeDtypeStruct((MLA_HEADS, V_ROWS, TOKENS), BF16)],
        name="mla_kv_heads",
        compiler_params=pltpu.CompilerParams(vmem_limit_bytes=_vmem_limit(
            _nbytes((rows, KV_LORA), BF16), _nbytes((bt, KV_LORA), BF16), _nbytes((ROPE_DIM, bt), F32),
            _nbytes((QK_DIM, bt), F32), _nbytes((heads, bt, QK_PAD), BF16), _nbytes((heads, V_DIM, bt), BF16),
            scratch=3 * _nbytes((rows, bt), F32))),
    )(w_kvb_t, kvn, kpe_t, g, cos_t, sin_t)


def _gated_store(y_ref, a, qi, o_t, sg_ref):
    rows = slice(qi * ATT_BLOCK, (qi + 1) * ATT_BLOCK)
    cols = slice(a * o_t.shape[0], (a + 1) * o_t.shape[0])
    y_ref[rows, cols] = (o_t.T * sg_ref[rows, cols].astype(F32)).astype(BF16)


ROW_CHUNK = 32


def _sublane_groups(x):
    return x.reshape(x.shape[0] // V7X_SUBLANES, V7X_SUBLANES, x.shape[1])


ATT_HEADS_PER_STEP = 2


def _mla_attn_kernel(q_t_ref, k_ref, v_t_ref, sg_ref, y_ref, *p_refs):
    blk = ATT_BLOCK
    causal = (lax.broadcasted_iota(jnp.int32, (blk, blk), 0)
              <= lax.broadcasted_iota(jnp.int32, (blk, blk), 1))

    def score_tile(a, qi, kj):
        s = jnp.dot(k_ref[a, kj * blk:(kj + 1) * blk, :], q_t_ref[a, :, qi * blk:(qi + 1) * blk],
                    preferred_element_type=F32)
        if kj == qi:
            s = jnp.where(causal, s, -jnp.inf)
        return s, jnp.max(_sublane_groups(s), axis=0)

    blocks = [(a, qi) for qi in range(N_ATT_BLOCKS - 1, -1, -1) for a in range(ATT_HEADS_PER_STEP)]
    tiles = [score_tile(*blocks[0], kj) for kj in range(blocks[0][1] + 1)]
    for bi, (a, qi) in enumerate(blocks):
        p_ref = p_refs[a * N_ATT_BLOCKS + qi]
        n_cur = qi + 1
        nxt = blocks[bi + 1] if bi + 1 < len(blocks) else None
        n_next = nxt[1] + 1 if nxt else 0
        m8 = tiles[0][1]
        for _, tile_max in tiles[1:]:
            m8 = jnp.maximum(m8, tile_max)
        m = jnp.max(m8, axis=0, keepdims=True)
        next_tiles = []
        for kj in range(n_cur):
            while len(next_tiles) * n_cur < (kj + 1) * n_next:
                next_tiles.append(score_tile(*nxt, len(next_tiles)))
            s = tiles[kj][0]
            for r in range(0, blk, ROW_CHUNK):
                p = jnp.exp2(s[r:r + ROW_CHUNK] - m)
                p_ref[kj * blk + r:kj * blk + r + ROW_CHUNK, :] = p.astype(BF16)
        pv = jnp.dot(v_t_ref[a, :, 0:n_cur * blk], p_ref[...], preferred_element_type=F32)
        _gated_store(y_ref, a, qi, pv[:V_DIM] / pv[V_DIM:V_DIM + 1], sg_ref)
        tiles = next_tiles


def mla_attention(q_t, k, v_t, sg):
    blk = ATT_BLOCK
    hp = ATT_HEADS_PER_STEP
    p_scratch = [pltpu.VMEM(((qi + 1) * blk, blk), BF16) for _ in range(hp) for qi in range(N_ATT_BLOCKS)]
    return pl.pallas_call(
        _mla_attn_kernel, grid=(BATCH, MLA_HEADS // hp),
        in_specs=[pl.BlockSpec((hp, QK_PAD, SEQ), lambda b, h: (h, 0, b)),
                  pl.BlockSpec((hp, SEQ, QK_PAD), lambda b, h: (h, b, 0)),
                  pl.BlockSpec((hp, V_ROWS, SEQ), lambda b, h: (h, 0, b)),
                  pl.BlockSpec((SEQ, hp * V_DIM), lambda b, h: (b, h))],
        out_specs=pl.BlockSpec((SEQ, hp * V_DIM), lambda b, h: (b, h)),
        out_shape=jax.ShapeDtypeStruct((TOKENS, MLA_WIDTH), BF16),
        scratch_shapes=p_scratch, name="mla_attention",
        compiler_params=pltpu.CompilerParams(vmem_limit_bytes=_vmem_limit(
            hp * _nbytes((QK_PAD, SEQ), BF16), hp * _nbytes((SEQ, QK_PAD), BF16),
            hp * _nbytes((V_ROWS, SEQ), BF16), 2 * hp * _nbytes((SEQ, V_DIM), BF16),
            scratch=hp * 24 * _nbytes((blk, blk), F32))),
    )(q_t, k, v_t, sg)


def _sublane_suffix_products(tot):
    sub = lax.broadcasted_iota(jnp.int32, tot.shape, 0)
    one = jnp.ones_like(tot)

    def shifted(x, k):
        return jnp.where(sub < V7X_SUBLANES - k, pltpu.roll(x, V7X_SUBLANES - k, axis=0), one)

    e = shifted(tot, 1)
    e = e * shifted(e, 1)
    e = e * shifted(e, 2)
    e = e * shifted(e, 4)
    full = tot * e
    return e, jnp.broadcast_to(full[0:1], tot.shape)


def _product(xs):
    xs = list(xs)
    while len(xs) > 1:
        xs = [xs[i] * xs[i + 1] for i in range(0, len(xs), 2)]
    return xs[0]


def _sb_attn_kernel(q_t_ref, k_ref, v_t_ref, sg_ref, y_ref, *scratch):
    blk = ATT_BLOCK
    groups = blk // V7X_SUBLANES
    per_chunk = ROW_CHUNK // V7X_SUBLANES
    chunks = blk // ROW_CHUNK
    g_refs, w_refs = scratch[:N_ATT_BLOCKS], scratch[N_ATT_BLOCKS:]
    slack = (lax.broadcasted_iota(jnp.int32, (V7X_SUBLANES, blk), 1)
             - groups * lax.broadcasted_iota(jnp.int32, (V7X_SUBLANES, blk), 0))

    def half_scores(qi, kj):
        return jnp.dot(k_ref[kj * blk:(kj + 1) * blk, :], q_t_ref[:, qi * blk:(qi + 1) * blk],
                       preferred_element_type=F32)

    order = [(qi, kj) for qi in range(N_ATT_BLOCKS - 1, -1, -1) for kj in range(qi, -1, -1)]
    zh_next = half_scores(*order[0])
    carry = None
    for step, (qi, kj) in enumerate(order):
        g_ref, w_ref = g_refs[qi], w_refs[qi]
        zh = zh_next
        if step + 1 < len(order):
            zh_next = half_scores(*order[step + 1])
        if kj == qi:
            carry = None
        chunk_total = []
        for c in range(chunks):
            rows = slice(c * ROW_CHUNK, (c + 1) * ROW_CHUNK)
            g3 = _sublane_groups(0.5 - 0.5 * jnp.tanh(zh[rows]))
            gs = [g3[i] for i in range(per_chunk)]
            if kj == qi:
                gs = [jnp.where(slack > c * per_chunk + i, gs[i], 1.0) for i in range(per_chunk)]
            g_ref[rows, :] = jnp.concatenate(gs, axis=0)
            chunk_total.append(_product(gs))
        later_sublanes, block_total = _sublane_suffix_products(_product(chunk_total))
        run = later_sublanes if carry is None else carry * later_sublanes
        chunk_run = [None] * chunks
        for c in reversed(range(chunks)):
            chunk_run[c] = run
            run = run * chunk_total[c]
        for c in range(chunks):
            g3 = _sublane_groups(g_ref[c * ROW_CHUNK:(c + 1) * ROW_CHUNK, :])
            run = chunk_run[c]
            w = [None] * per_chunk
            for i in reversed(range(per_chunk)):
                before = run * g3[i]
                w[i] = run - before
                run = before
            r0 = kj * blk + c * ROW_CHUNK
            w_ref[r0:r0 + ROW_CHUNK, :] = jnp.concatenate(w, axis=0).astype(BF16)
        carry = block_total if carry is None else carry * block_total
        if kj == 0:
            pv = jnp.dot(v_t_ref[:, 0:(qi + 1) * blk], w_ref[...], preferred_element_type=F32)
            _gated_store(y_ref, 0, qi, pv, sg_ref)


def sb_attention(q_t, k, v_t, sg):
    blk = ATT_BLOCK
    scratch = ([pltpu.VMEM((blk, blk), F32) for _ in range(N_ATT_BLOCKS)]
               + [pltpu.VMEM(((qi + 1) * blk, blk), BF16) for qi in range(N_ATT_BLOCKS)])
    return pl.pallas_call(
        _sb_attn_kernel, grid=(BATCH, SB_HEADS),
        in_specs=[pl.BlockSpec((SB_DIM, SEQ), lambda b, h: (h, b)),
                  pl.BlockSpec((SEQ, SB_DIM), lambda b, h: (b, h)),
                  pl.BlockSpec((SB_DIM, SEQ), lambda b, h: (h, b)),
                  pl.BlockSpec((SEQ, SB_DIM), lambda b, h: (b, h))],
        out_specs=pl.BlockSpec((SEQ, SB_DIM), lambda b, h: (b, h)),
        out_shape=jax.ShapeDtypeStruct((TOKENS, SB_WIDTH), BF16),
        scratch_shapes=scratch, name="sb_attention",
        compiler_params=pltpu.CompilerParams(vmem_limit_bytes=_vmem_limit(
            _nbytes((SB_DIM, SEQ), BF16), _nbytes((SEQ, SB_DIM), BF16), _nbytes((SB_DIM, SEQ), BF16),
            2 * _nbytes((SEQ, SB_DIM), BF16), scratch=30 * _nbytes((blk, blk), F32))),
    )(q_t, k, v_t, sg)


def _mla_layer(h, j, norm_g, w_in, q_a_g, w_qb, kv_a_g, w_kvb, q_norm_g, k_norm_g, w_out_all, cos_t, sin_t):
    nlat = Q_LORA + KV_LORA
    w_lat = w_in[:, :nlat].astype(BF16)
    w_pe_t = w_in[:, nlat:nlat + ROPE_DIM].T.astype(BF16)
    w_gate = w_in[:, nlat + ROPE_DIM:].astype(BF16)
    hn = rms_norm_tm(h, norm_g)
    sg = project(hn, w_gate, kind="silu")
    qn, kvn, kpe_t = mla_latents(hn, w_lat, w_pe_t, q_a_g, kv_a_g)
    q_t = mla_q_heads(w_qb.T.astype(BF16), qn, q_norm_g, cos_t, sin_t)
    k, v_t = mla_kv_heads(w_kvb.T.astype(BF16), kvn, kpe_t, k_norm_g, cos_t, sin_t)
    y = mla_attention(q_t, k, v_t, sg)
    return project(y, w_out_all, layer=j, n=D_MODEL, kind="resid", resid=h)


def _sb_layer(h, j, norm_g, w_in_all, w_out_all):
    w = SB_WIDTH
    hn, hn_perm = rms_norm_tm(h, norm_g, with_perm=True)
    q_t = project(hn, w_in_all, layer=j, col0=0, n=w, kind="fm", scale=0.5 * SB_DIM ** -0.5)
    k = project(hn_perm, w_in_all, layer=j, col0=w, n=w, kind="plain")
    v_t = project(hn_perm, w_in_all, layer=j, col0=2 * w, n=w, kind="fm")
    sg = project(hn, w_in_all, layer=j, col0=3 * w, n=w, kind="silu")
    y = sb_attention(q_t, k, v_t, sg)
    return project(y, w_out_all, layer=j, n=D_MODEL, kind="resid", resid=h)


def kernel(x, mla_norm_g, mla_w_in, mla_q_a_g, mla_w_qb, mla_kv_a_g, mla_w_kvb, mla_q_norm_g,
           mla_k_norm_g, mla_w_out, sb_norm_g, sb_w_in, sb_w_out):
    cos_t, sin_t = rope_tables_fm()
    h = x.reshape(TOKENS, D_MODEL)
    for i in range(DEPTH):
        j = i // 2
        if i % 2 == 0:
            h = _mla_layer(h, j, mla_norm_g[j], mla_w_in[j], mla_q_a_g[j], mla_w_qb[j], mla_kv_a_g[j],
                           mla_w_kvb[j], mla_q_norm_g[j], mla_k_norm_g[j], mla_w_out, cos_t, sin_t)
        else:
            h = _sb_layer(h, j, sb_norm_g[j], sb_w_in, sb_w_out)
    return h.reshape(BATCH, SEQ, D_MODEL)
```

```python
import functools

import numpy as np
import jax
import jax.numpy as jnp
from jax import lax
from jax.experimental import pallas as pl
from jax.experimental.pallas import tpu as pltpu

D_MODEL = 4096
BATCH = 4
SEQ = 2048
DEPTH = 4
MLA_HEADS = 32
Q_LORA = 1024
KV_LORA = 512
NOPE_DIM = 128
ROPE_DIM = 64
V_DIM = 128
QK_DIM = NOPE_DIM + ROPE_DIM
MLA_WIDTH = MLA_HEADS * V_DIM
ROPE_THETA = 10000.0
SB_HEADS = 32
SB_DIM = 128
SB_WIDTH = SB_HEADS * SB_DIM
EPS = 1e-6
TOKENS = BATCH * SEQ

V7X_SUBLANES = 8
V7X_LANES = 128
V7X_MXU_DIM = 256
V7X_VMEM_BYTES = 64 * 1024 * 1024
VMEM_LIMIT_CAP = V7X_VMEM_BYTES - 8 * 1024 * 1024

QK_PAD = V7X_MXU_DIM
V_ROWS = V_DIM + 2 * V7X_SUBLANES
ATT_BLOCK = 512
N_ATT_BLOCKS = SEQ // ATT_BLOCK

BF16 = jnp.bfloat16
F32 = jnp.float32
NT_DIMS = (((1,), (1,)), ((), ()))


def _vmem_limit(*block_bytes, scratch=0):
    need = 2 * sum(block_bytes) + scratch
    return int(min(VMEM_LIMIT_CAP, need + 8 * 1024 * 1024))


def _nbytes(shape, dtype):
    return int(np.prod(shape)) * jnp.dtype(dtype).itemsize


def _rms_norm_rows(x, g):
    ms = jnp.mean(x * x, axis=-1, keepdims=True)
    return (x * lax.rsqrt(ms + EPS) * g).astype(BF16)


def _norm_perm_kernel(x_ref, g_ref, perm_ref, o_ref, operm_ref):
    y = _rms_norm_rows(x_ref[...], g_ref[...])
    o_ref[...] = y
    operm_ref[...] = jnp.dot(perm_ref[...], y, preferred_element_type=F32).astype(BF16)


def _scan_perm_matrix():
    r = np.arange(ATT_BLOCK)
    src = (r % V7X_SUBLANES) * (ATT_BLOCK // V7X_SUBLANES) + r // V7X_SUBLANES
    p = np.zeros((ATT_BLOCK, ATT_BLOCK), np.float32)
    p[r, src] = 1.0
    return jnp.asarray(p, BF16)


def rms_norm_perm(x, g):
    bt = ATT_BLOCK
    row_spec = pl.BlockSpec((bt, D_MODEL), lambda i: (i, 0))
    o_shape = jax.ShapeDtypeStruct((TOKENS, D_MODEL), BF16)
    xb = _nbytes((bt, D_MODEL), F32)
    ob = _nbytes((bt, D_MODEL), BF16)
    return pl.pallas_call(
        _norm_perm_kernel, grid=(TOKENS // bt,),
        in_specs=[row_spec, pl.BlockSpec((1, D_MODEL), lambda i: (0, 0)),
                  pl.BlockSpec((bt, bt), lambda i: (0, 0))],
        out_specs=[row_spec, row_spec], out_shape=[o_shape, o_shape], name="rms_norm_perm",
        compiler_params=pltpu.CompilerParams(vmem_limit_bytes=_vmem_limit(xb, ob, ob, scratch=3 * xb)),
    )(x, g.reshape(1, D_MODEL), _scan_perm_matrix())


def _proj_kernel(*refs, kind, scale, cast):
    if kind == "resid":
        a_ref, w_ref, r_ref, o_ref = refs[:4]
        scratch = refs[4:]
    else:
        a_ref, w_ref, o_ref = refs[:3]
        scratch = refs[3:]
    if cast:
        (w_bf_ref,) = scratch

        @pl.when(pl.program_id(1) == 0)
        def _cast_weights():
            w_bf_ref[...] = w_ref[...].astype(BF16)

        w = w_bf_ref[...]
    else:
        w = w_ref[...]
    acc = jnp.dot(a_ref[...], w, preferred_element_type=F32)
    if kind == "resid":
        o_ref[...] = r_ref[...] + acc
    elif kind == "silu":
        o_ref[...] = (acc * jax.nn.sigmoid(acc)).astype(BF16)
    elif kind == "plain":
        o_ref[...] = acc.astype(BF16)
    else:
        assert kind == "fm"
        o_ref[...] = (acc * scale).T.astype(BF16) if scale != 1.0 else acc.T.astype(BF16)


def project(a, w, *, kind, layer=None, col0=0, n=None, scale=1.0, resid=None, bm=1024, bn=512):
    m, k = a.shape
    cast = w.ndim == 3
    if cast:
        assert col0 % bn == 0
        w_spec = pl.BlockSpec((None, k, bn), lambda j, i: (layer, 0, col0 // bn + j))
    else:
        n = w.shape[1]
        w_spec = pl.BlockSpec((k, bn), lambda j, i: (0, j))
    in_specs = [pl.BlockSpec((bm, k), lambda j, i: (i, 0)), w_spec]
    args = [a, w]
    out_bytes = _nbytes((bm, bn), BF16)
    if kind == "resid":
        in_specs.append(pl.BlockSpec((bm, bn), lambda j, i: (i, j)))
        args.append(resid)
        out_spec = pl.BlockSpec((bm, bn), lambda j, i: (i, j))
        out_shape = jax.ShapeDtypeStruct((m, n), F32)
        out_bytes = 2 * _nbytes((bm, bn), F32)
    elif kind == "fm":
        out_spec = pl.BlockSpec((bn, bm), lambda j, i: (j, i))
        out_shape = jax.ShapeDtypeStruct((n, m), BF16)
    else:
        out_spec = pl.BlockSpec((bm, bn), lambda j, i: (i, j))
        out_shape = jax.ShapeDtypeStruct((m, n), BF16)
    return pl.pallas_call(
        functools.partial(_proj_kernel, kind=kind, scale=scale, cast=cast),
        grid=(n // bn, m // bm), in_specs=in_specs, out_specs=out_spec, out_shape=out_shape,
        scratch_shapes=[pltpu.VMEM((k, bn), BF16)] if cast else [],
        name="proj_" + kind,
        compiler_params=pltpu.CompilerParams(vmem_limit_bytes=_vmem_limit(
            _nbytes((bm, k), BF16), _nbytes((k, bn), F32 if cast else BF16), out_bytes,
            scratch=_nbytes((k, bn), BF16) + 3 * _nbytes((bm, bn), F32))),
    )(*args)


def _rope_kernel(inv_ref, cos_ref, sin_ref):
    pos = lax.broadcasted_iota(jnp.int32, cos_ref.shape, 1).astype(F32)
    ang = pos * inv_ref[...]
    cos_ref[...] = jnp.cos(ang)
    sin_ref[...] = jnp.sin(ang)


def rope_tables_fm():
    half = ROPE_DIM // 2
    inv = 1.0 / (np.float32(ROPE_THETA) ** (np.arange(0, ROPE_DIM, 2, dtype=np.float32) / np.float32(ROPE_DIM)))
    inv = jnp.asarray(np.broadcast_to(inv.astype(np.float32)[:, None], (half, SEQ)))
    shape = jax.ShapeDtypeStruct((half, SEQ), F32)
    return pl.pallas_call(_rope_kernel, out_shape=[shape, shape], name="rope_tables")(inv)


def _rope_fm(x, cos, sin):
    half = ROPE_DIM // 2
    x1, x2 = x[:half], x[half:]
    return jnp.concatenate([x1 * cos - x2 * sin, x1 * sin + x2 * cos], axis=0)


def _mla_latent_kernel(h_ref, ng_ref, wlat_ref, wpe_t_ref, qg_ref, kvg_ref,
                       hn_ref, qn_ref, kvn_ref, kpe_t_ref):
    hn = _rms_norm_rows(h_ref[...], ng_ref[...])
    hn_ref[...] = hn
    acc = jnp.dot(hn, wlat_ref[...], preferred_element_type=F32)
    ql = acc[:, :Q_LORA]
    kvl = acc[:, Q_LORA:]
    qn_ref[...] = (ql * lax.rsqrt(jnp.mean(ql * ql, axis=-1, keepdims=True) + EPS) * qg_ref[...]).astype(BF16)
    kvn_ref[...] = (kvl * lax.rsqrt(jnp.mean(kvl * kvl, axis=-1, keepdims=True) + EPS) * kvg_ref[...]).astype(BF16)
    kpe_t_ref[...] = lax.dot_general(wpe_t_ref[...], hn, NT_DIMS, preferred_element_type=F32)


def mla_latents(h, norm_g, w_lat, w_pe_t, q_a_g, kv_a_g, *, bt=512):
    nlat = Q_LORA + KV_LORA
    once = pl.Buffered(1)
    return pl.pallas_call(
        _mla_latent_kernel, grid=(TOKENS // bt,),
        in_specs=[pl.BlockSpec((bt, D_MODEL), lambda i: (i, 0)),
                  pl.BlockSpec((1, D_MODEL), lambda i: (0, 0), pipeline_mode=once),
                  pl.BlockSpec((D_MODEL, nlat), lambda i: (0, 0), pipeline_mode=once),
                  pl.BlockSpec((ROPE_DIM, D_MODEL), lambda i: (0, 0), pipeline_mode=once),
                  pl.BlockSpec((1, Q_LORA), lambda i: (0, 0), pipeline_mode=once),
                  pl.BlockSpec((1, KV_LORA), lambda i: (0, 0), pipeline_mode=once)],
        out_specs=[pl.BlockSpec((bt, D_MODEL), lambda i: (i, 0)),
                   pl.BlockSpec((bt, Q_LORA), lambda i: (i, 0)),
                   pl.BlockSpec((bt, KV_LORA), lambda i: (i, 0)),
                   pl.BlockSpec((ROPE_DIM, bt), lambda i: (0, i))],
        out_shape=[jax.ShapeDtypeStruct((TOKENS, D_MODEL), BF16),
                   jax.ShapeDtypeStruct((TOKENS, Q_LORA), BF16),
                   jax.ShapeDtypeStruct((TOKENS, KV_LORA), BF16),
                   jax.ShapeDtypeStruct((ROPE_DIM, TOKENS), F32)],
        name="mla_latents",
        compiler_params=pltpu.CompilerParams(vmem_limit_bytes=_vmem_limit(
            _nbytes((bt, D_MODEL), F32), _nbytes((bt, D_MODEL), BF16), _nbytes((bt, nlat), BF16),
            _nbytes((ROPE_DIM, bt), F32),
            scratch=_nbytes((D_MODEL, nlat), BF16) + _nbytes((ROPE_DIM, D_MODEL), BF16)
            + 2 * _nbytes((bt, D_MODEL), F32) + 2 * _nbytes((bt, nlat), F32))),
    )(h, norm_g.reshape(1, D_MODEL), w_lat, w_pe_t, q_a_g.reshape(1, Q_LORA), kv_a_g.reshape(1, KV_LORA))


def _transposed_weights(w_ref, w_t_ref):
    @pl.when(pl.program_id(1) == 0)
    def _transpose():
        w_t_ref[...] = w_ref[...].T.astype(BF16)

    return w_t_ref[...]


HEAD_SPLITS = 4


def _mla_q_kernel(w_ref, qn_ref, g_ref, cos_ref, sin_ref, q_t_ref, w_t_ref, *, heads, scale):
    w_t = _transposed_weights(w_ref, w_t_ref)
    qn_rows = qn_ref[...]
    g = g_ref[...]
    cos = cos_ref[...]
    sin = sin_ref[...]
    bt = qn_rows.shape[0]
    pad = jnp.zeros((QK_PAD - QK_DIM, bt), F32)
    sub = heads // HEAD_SPLITS
    accs = [lax.dot_general(w_t[c * sub * QK_DIM:(c + 1) * sub * QK_DIM], qn_rows, NT_DIMS,
                            preferred_element_type=F32) for c in range(HEAD_SPLITS)]
    for h in range(heads):
        q = accs[h // sub][(h % sub) * QK_DIM:(h % sub + 1) * QK_DIM]
        r = lax.rsqrt(jnp.mean(q * q, axis=0, keepdims=True) + EPS)
        qn = q * r * g
        out = jnp.concatenate([qn[:NOPE_DIM], _rope_fm(qn[NOPE_DIM:], cos, sin), pad], axis=0)
        q_t_ref[h] = (out * scale).astype(BF16)


def mla_q_heads(w_qb_all, layer, qn, q_norm_g, cos_t, sin_t, *, heads=8, bt=512):
    rows = heads * QK_DIM
    g = jnp.broadcast_to(q_norm_g[:, None], (QK_DIM, bt))
    nseq = SEQ // bt
    return pl.pallas_call(
        functools.partial(_mla_q_kernel, heads=heads, scale=QK_DIM ** -0.5 * np.log2(np.e)),
        grid=(MLA_HEADS // heads, TOKENS // bt),
        in_specs=[pl.BlockSpec((None, Q_LORA, rows), lambda j, i: (layer, 0, j)),
                  pl.BlockSpec((bt, Q_LORA), lambda j, i: (i, 0)),
                  pl.BlockSpec((QK_DIM, bt), lambda j, i: (0, 0)),
                  pl.BlockSpec((ROPE_DIM // 2, bt), lambda j, i: (0, i % nseq)),
                  pl.BlockSpec((ROPE_DIM // 2, bt), lambda j, i: (0, i % nseq))],
        out_specs=pl.BlockSpec((heads, QK_PAD, bt), lambda j, i: (j, 0, i)),
        out_shape=jax.ShapeDtypeStruct((MLA_HEADS, QK_PAD, TOKENS), BF16),
        scratch_shapes=[pltpu.VMEM((rows, Q_LORA), BF16)], name="mla_q_heads",
        compiler_params=pltpu.CompilerParams(vmem_limit_bytes=_vmem_limit(
            _nbytes((Q_LORA, rows), F32), _nbytes((bt, Q_LORA), BF16), _nbytes((QK_DIM, bt), F32),
            _nbytes((heads, QK_PAD, bt), BF16),
            scratch=_nbytes((rows, Q_LORA), BF16) + 3 * _nbytes((rows, bt), F32))),
    )(w_qb_all, qn, g, cos_t, sin_t)


def _mla_kv_kernel(w_ref, kvn_ref, kpe_t_ref, g_ref, cos_ref, sin_ref, k_ref, v_t_ref, w_t_ref, *, heads):
    w_t = _transposed_weights(w_ref, w_t_ref)
    kvn_rows = kvn_ref[...]
    g = g_ref[...]
    bt = kvn_rows.shape[0]
    kpe = kpe_t_ref[...]
    ss_pe = jnp.sum(kpe * kpe, axis=0, keepdims=True)
    kpe_rot = _rope_fm(kpe * g[NOPE_DIM:], cos_ref[...], sin_ref[...])
    pad = jnp.zeros((QK_PAD - QK_DIM, bt), F32)
    per_head = NOPE_DIM + V_DIM
    sub = heads // HEAD_SPLITS
    accs = [lax.dot_general(w_t[c * sub * per_head:(c + 1) * sub * per_head], kvn_rows, NT_DIMS,
                            preferred_element_type=F32) for c in range(HEAD_SPLITS)]
    for h in range(heads):
        acc = accs[h // sub][(h % sub) * per_head:(h % sub + 1) * per_head]
        kn = acc[:NOPE_DIM]
        v = acc[NOPE_DIM:]
        ss = jnp.sum(kn * kn, axis=0, keepdims=True) + ss_pe
        r = lax.rsqrt(ss * (1.0 / QK_DIM) + EPS)
        k_t = jnp.concatenate([kn * g[:NOPE_DIM] * r, kpe_rot * r, pad], axis=0)
        k_ref[h] = k_t.T.astype(BF16)
        v_t_ref[h, :V_DIM] = v.astype(BF16)
        v_t_ref[h, V_DIM:] = jnp.ones((V_ROWS - V_DIM, bt), BF16)


def mla_kv_heads(w_kvb_all, layer, kvn, kpe_t, k_norm_g, cos_t, sin_t, *, heads=4):
    bt = ATT_BLOCK
    rows = heads * (NOPE_DIM + V_DIM)
    g = jnp.broadcast_to(k_norm_g[:, None], (QK_DIM, bt))
    nseq = SEQ // bt
    return pl.pallas_call(
        functools.partial(_mla_kv_kernel, heads=heads),
        grid=(MLA_HEADS // heads, TOKENS // bt),
        in_specs=[pl.BlockSpec((None, KV_LORA, rows), lambda j, i: (layer, 0, j)),
                  pl.BlockSpec((bt, KV_LORA), lambda j, i: (i, 0)),
                  pl.BlockSpec((ROPE_DIM, bt), lambda j, i: (0, i)),
                  pl.BlockSpec((QK_DIM, bt), lambda j, i: (0, 0)),
                  pl.BlockSpec((ROPE_DIM // 2, bt), lambda j, i: (0, i % nseq)),
                  pl.BlockSpec((ROPE_DIM // 2, bt), lambda j, i: (0, i % nseq))],
        out_specs=[pl.BlockSpec((heads, bt, QK_PAD), lambda j, i: (j, i, 0)),
                   pl.BlockSpec((heads, V_ROWS, bt), lambda j, i: (j, 0, i))],
        out_shape=[jax.ShapeDtypeStruct((MLA_HEADS, TOKENS, QK_PAD), BF16),
                   jax.ShapeDtypeStruct((MLA_HEADS, V_ROWS, TOKENS), BF16)],
        scratch_shapes=[pltpu.VMEM((rows, KV_LORA), BF16)], name="mla_kv_heads",
        compiler_params=pltpu.CompilerParams(vmem_limit_bytes=_vmem_limit(
            _nbytes((KV_LORA, rows), F32), _nbytes((bt, KV_LORA), BF16), _nbytes((ROPE_DIM, bt), F32),
            _nbytes((QK_DIM, bt), F32), _nbytes((heads, bt, QK_PAD), BF16), _nbytes((heads, V_ROWS, bt), BF16),
            scratch=_nbytes((rows, KV_LORA), BF16) + 3 * _nbytes((rows, bt), F32))),
    )(w_kvb_all, kvn, kpe_t, g, cos_t, sin_t)


def _gated_store(y_ref, a, qi, o_t, sg_ref):
    rows = slice(qi * ATT_BLOCK, (qi + 1) * ATT_BLOCK)
    cols = slice(a * o_t.shape[0], (a + 1) * o_t.shape[0])
    y_ref[rows, cols] = (o_t.T * sg_ref[rows, cols].astype(F32)).astype(BF16)


ROW_CHUNK = 32


def _sublane_groups(x):
    return x.reshape(x.shape[0] // V7X_SUBLANES, V7X_SUBLANES, x.shape[1])


ATT_HEADS_PER_STEP = 2


def _mla_attn_kernel(q_t_ref, k_ref, v_t_ref, sg_ref, y_ref, *p_refs):
    blk = ATT_BLOCK
    causal = (lax.broadcasted_iota(jnp.int32, (blk, blk), 0)
              <= lax.broadcasted_iota(jnp.int32, (blk, blk), 1))

    def score_tile(a, qi, kj):
        s = jnp.dot(k_ref[a, kj * blk:(kj + 1) * blk, :], q_t_ref[a, :, qi * blk:(qi + 1) * blk],
                    preferred_element_type=F32)
        if kj == qi:
            s = jnp.where(causal, s, -jnp.inf)
        return s, jnp.max(_sublane_groups(s), axis=0)

    blocks = [(a, qi) for qi in range(N_ATT_BLOCKS - 1, -1, -1) for a in range(ATT_HEADS_PER_STEP)]
    tiles = [score_tile(*blocks[0], kj) for kj in range(blocks[0][1] + 1)]
    for bi, (a, qi) in enumerate(blocks):
        p_ref = p_refs[a * N_ATT_BLOCKS + qi]
        n_cur = qi + 1
        nxt = blocks[bi + 1] if bi + 1 < len(blocks) else None
        n_next = nxt[1] + 1 if nxt else 0
        m8 = tiles[0][1]
        for _, tile_max in tiles[1:]:
            m8 = jnp.maximum(m8, tile_max)
        m = jnp.max(m8, axis=0, keepdims=True)
        next_tiles = []
        for kj in range(n_cur):
            while len(next_tiles) * n_cur < (kj + 1) * n_next:
                next_tiles.append(score_tile(*nxt, len(next_tiles)))
            s = tiles[kj][0]
            for r in range(0, blk, ROW_CHUNK):
                p = jnp.exp2(s[r:r + ROW_CHUNK] - m)
                p_ref[kj * blk + r:kj * blk + r + ROW_CHUNK, :] = p.astype(BF16)
        pv = jnp.dot(v_t_ref[a, :, 0:n_cur * blk], p_ref[...], preferred_element_type=F32)
        _gated_store(y_ref, a, qi, pv[:V_DIM] / pv[V_DIM:V_DIM + 1], sg_ref)
        tiles = next_tiles


def mla_attention(q_t, k, v_t, sg):
    blk = ATT_BLOCK
    hp = ATT_HEADS_PER_STEP
    p_scratch = [pltpu.VMEM(((qi + 1) * blk, blk), BF16) for _ in range(hp) for qi in range(N_ATT_BLOCKS)]
    return pl.pallas_call(
        _mla_attn_kernel, grid=(BATCH, MLA_HEADS // hp),
        in_specs=[pl.BlockSpec((hp, QK_PAD, SEQ), lambda b, h: (h, 0, b)),
                  pl.BlockSpec((hp, SEQ, QK_PAD), lambda b, h: (h, b, 0)),
                  pl.BlockSpec((hp, V_ROWS, SEQ), lambda b, h: (h, 0, b)),
                  pl.BlockSpec((SEQ, hp * V_DIM), lambda b, h: (b, h))],
        out_specs=pl.BlockSpec((SEQ, hp * V_DIM), lambda b, h: (b, h)),
        out_shape=jax.ShapeDtypeStruct((TOKENS, MLA_WIDTH), BF16),
        scratch_shapes=p_scratch, name="mla_attention",
        compiler_params=pltpu.CompilerParams(vmem_limit_bytes=_vmem_limit(
            hp * _nbytes((QK_PAD, SEQ), BF16), hp * _nbytes((SEQ, QK_PAD), BF16),
            hp * _nbytes((V_ROWS, SEQ), BF16), 2 * hp * _nbytes((SEQ, V_DIM), BF16),
            scratch=hp * 24 * _nbytes((blk, blk), F32))),
    )(q_t, k, v_t, sg)


def _sublane_suffix_products(tot):
    sub = lax.broadcasted_iota(jnp.int32, tot.shape, 0)
    one = jnp.ones_like(tot)

    def shifted(x, k):
        return jnp.where(sub < V7X_SUBLANES - k, pltpu.roll(x, V7X_SUBLANES - k, axis=0), one)

    e = shifted(tot, 1)
    e = e * shifted(e, 1)
    e = e * shifted(e, 2)
    e = e * shifted(e, 4)
    full = tot * e
    return e, jnp.broadcast_to(full[0:1], tot.shape)


def _product(xs):
    xs = list(xs)
    while len(xs) > 1:
        xs = [xs[i] * xs[i + 1] for i in range(0, len(xs), 2)]
    return xs[0]


def _sb_attn_kernel(q_t_ref, k_ref, v_t_ref, sg_ref, y_ref, *scratch):
    blk = ATT_BLOCK
    groups = blk // V7X_SUBLANES
    per_chunk = ROW_CHUNK // V7X_SUBLANES
    chunks = blk // ROW_CHUNK
    g_refs, w_refs = scratch[:2], scratch[2:]
    slack = (lax.broadcasted_iota(jnp.int32, (V7X_SUBLANES, blk), 1)
             - groups * lax.broadcasted_iota(jnp.int32, (V7X_SUBLANES, blk), 0))

    def head(a):
        return slice(a * SB_DIM, (a + 1) * SB_DIM)

    def half_scores(a, qi, kj):
        return jnp.dot(k_ref[kj * blk:(kj + 1) * blk, head(a)], q_t_ref[head(a), qi * blk:(qi + 1) * blk],
                       preferred_element_type=F32)

    order = [(a, qi, kj) for qi in range(N_ATT_BLOCKS - 1, -1, -1) for a in range(ATT_HEADS_PER_STEP)
             for kj in range(qi, -1, -1)]
    zh_next = half_scores(*order[0])
    carry = None
    for step, (a, qi, kj) in enumerate(order):
        g_ref, w_ref = g_refs[step % 2], w_refs[a * N_ATT_BLOCKS + qi]
        zh = zh_next
        if step + 1 < len(order):
            zh_next = half_scores(*order[step + 1])
        if kj == qi:
            carry = None
        chunk_total = []
        for c in range(chunks):
            rows = slice(c * ROW_CHUNK, (c + 1) * ROW_CHUNK)
            g3 = _sublane_groups(0.5 - 0.5 * jnp.tanh(zh[rows]))
            gs = [g3[i] for i in range(per_chunk)]
            if kj == qi:
                gs = [jnp.where(slack > c * per_chunk + i, gs[i], 1.0) for i in range(per_chunk)]
            g_ref[rows, :] = jnp.concatenate(gs, axis=0)
            chunk_total.append(_product(gs))
        later_sublanes, block_total = _sublane_suffix_products(_product(chunk_total))
        run = later_sublanes if carry is None else carry * later_sublanes
        chunk_run = [None] * chunks
        for c in reversed(range(chunks)):
            chunk_run[c] = run
            run = run * chunk_total[c]
        for c in range(chunks):
            g3 = _sublane_groups(g_ref[c * ROW_CHUNK:(c + 1) * ROW_CHUNK, :])
            run = chunk_run[c]
            w = [None] * per_chunk
            for i in reversed(range(per_chunk)):
                before = run * g3[i]
                w[i] = run - before
                run = before
            r0 = kj * blk + c * ROW_CHUNK
            w_ref[r0:r0 + ROW_CHUNK, :] = jnp.concatenate(w, axis=0).astype(BF16)
        carry = block_total if carry is None else carry * block_total
        if kj == 0:
            pv = jnp.dot(v_t_ref[head(a), 0:(qi + 1) * blk], w_ref[...], preferred_element_type=F32)
            _gated_store(y_ref, a, qi, pv, sg_ref)


def sb_attention(q_t, k, v_t, sg):
    blk = ATT_BLOCK
    hp = ATT_HEADS_PER_STEP
    scratch = ([pltpu.VMEM((blk, blk), F32) for _ in range(2)]
               + [pltpu.VMEM(((qi + 1) * blk, blk), BF16) for _ in range(hp) for qi in range(N_ATT_BLOCKS)])
    return pl.pallas_call(
        _sb_attn_kernel, grid=(BATCH, SB_HEADS // hp),
        in_specs=[pl.BlockSpec((hp * SB_DIM, SEQ), lambda b, h: (h, b)),
                  pl.BlockSpec((SEQ, hp * SB_DIM), lambda b, h: (b, h)),
                  pl.BlockSpec((hp * SB_DIM, SEQ), lambda b, h: (h, b)),
                  pl.BlockSpec((SEQ, hp * SB_DIM), lambda b, h: (b, h))],
        out_specs=pl.BlockSpec((SEQ, hp * SB_DIM), lambda b, h: (b, h)),
        out_shape=jax.ShapeDtypeStruct((TOKENS, SB_WIDTH), BF16),
        scratch_shapes=scratch, name="sb_attention",
        compiler_params=pltpu.CompilerParams(vmem_limit_bytes=_vmem_limit(
            hp * _nbytes((SB_DIM, SEQ), BF16), hp * _nbytes((SEQ, SB_DIM), BF16),
            hp * _nbytes((SB_DIM, SEQ), BF16), 2 * hp * _nbytes((SEQ, SB_DIM), BF16),
            scratch=hp * 24 * _nbytes((blk, blk), F32))),
    )(q_t, k, v_t, sg)


def _mla_layer(h, j, norm_g, w_in, q_a_g, w_qb_all, kv_a_g, w_kvb_all, q_norm_g, k_norm_g, w_out_all,
               cos_t, sin_t):
    nlat = Q_LORA + KV_LORA
    w_lat = w_in[:, :nlat].astype(BF16)
    w_pe_t = w_in[:, nlat:nlat + ROPE_DIM].T.astype(BF16)
    w_gate = w_in[:, nlat + ROPE_DIM:].astype(BF16)
    hn, qn, kvn, kpe_t = mla_latents(h, norm_g, w_lat, w_pe_t, q_a_g, kv_a_g)
    sg = project(hn, w_gate, kind="silu")
    q_t = mla_q_heads(w_qb_all, j, qn, q_norm_g, cos_t, sin_t)
    k, v_t = mla_kv_heads(w_kvb_all, j, kvn, kpe_t, k_norm_g, cos_t, sin_t)
    y = mla_attention(q_t, k, v_t, sg)
    return project(y, w_out_all, layer=j, n=D_MODEL, kind="resid", resid=h)


def _sb_layer(h, j, norm_g, w_in_all, w_out_all):
    w = SB_WIDTH
    hn, hn_perm = rms_norm_perm(h, norm_g)
    q_t = project(hn, w_in_all, layer=j, col0=0, n=w, kind="fm", scale=0.5 * SB_DIM ** -0.5)
    k = project(hn_perm, w_in_all, layer=j, col0=w, n=w, kind="plain")
    v_t = project(hn_perm, w_in_all, layer=j, col0=2 * w, n=w, kind="fm")
    sg = project(hn, w_in_all, layer=j, col0=3 * w, n=w, kind="silu")
    y = sb_attention(q_t, k, v_t, sg)
    return project(y, w_out_all, layer=j, n=D_MODEL, kind="resid", resid=h)


def kernel(x, mla_norm_g, mla_w_in, mla_q_a_g, mla_w_qb, mla_kv_a_g, mla_w_kvb, mla_q_norm_g,
           mla_k_norm_g, mla_w_out, sb_norm_g, sb_w_in, sb_w_out):
    cos_t, sin_t = rope_tables_fm()
    h = x.reshape(TOKENS, D_MODEL)
    for i in range(DEPTH):
        j = i // 2
        if i % 2 == 0:
            h = _mla_layer(h, j, mla_norm_g[j], mla_w_in[j], mla_q_a_g[j], mla_w_qb, mla_kv_a_g[j],
                           mla_w_kvb, mla_q_norm_g[j], mla_k_norm_g[j], mla_w_out, cos_t, sin_t)
        else:
            h = _sb_layer(h, j, sb_norm_g[j], sb_w_in, sb_w_out)
    return h.reshape(BATCH, SEQ, D_MODEL)
```
